```python
import jax, jax.numpy as jnp
from jax import lax
import numpy as np

D_MODEL = 1024
BATCH = 4
SEQ = 8192
DEPTH = 4

D_MIX = D_MODEL
N_HEADS_ATT = 8
HD_ATT = 64
D_ATT = N_HEADS_ATT * HD_ATT
DILATED_PATTERNS = ((128, 1), (512, 4), (2048, 16))
N_HEADS_ML = 4
HD_ML = 128
D_ML = N_HEADS_ML * HD_ML
CONV_W = 4
CHUNK = 64
N_GROUPS = 4
EXPERTS_PER_GROUP = 8
N_EXPERTS = N_GROUPS * EXPERTS_PER_GROUP
TOP_K = 2
D_EXPERT = 512
MOE_BLOCK = 128
EPS = 1e-6
SPLIT_SIZES = (D_ATT, D_ATT, D_ATT, D_ML, D_ML, D_ML, D_ML, N_HEADS_ML, N_HEADS_ML)
D_IN = sum(SPLIT_SIZES)
SPLIT_POINTS = tuple(int(v) for v in np.cumsum(SPLIT_SIZES)[:-1])

kernel_name = 'hymba_dilated_mlstm_hmoe_trunk'


def _rmsnorm(x, g):
    xf = x.astype(jnp.float32)
    y = xf * lax.rsqrt(jnp.mean(xf * xf, axis=-1, keepdims=True) + EPS)
    return (y * g.astype(jnp.float32)).astype(x.dtype)


def _dilated_window_branch(q, k, v, window, dilation):
    b, h, s, hd = q.shape
    band = window // dilation
    span = band * dilation
    s_pad = -(-s // span) * span
    length = s_pad // dilation
    n_blk = length // band

    def to_blocks(t):
        t = jnp.pad(t, ((0, 0), (0, 0), (0, s_pad - s), (0, 0)))
        t = t.reshape(b, h, length, dilation, hd).transpose(0, 1, 3, 2, 4)
        return t.reshape(b, h, dilation, n_blk, band, hd)

    def with_prev(t):
        prev = jnp.pad(t, ((0, 0), (0, 0), (0, 0), (1, 0), (0, 0), (0, 0)))[:, :, :, :-1]
        return jnp.concatenate([prev, t], axis=4)

    qb = to_blocks(q)
    kc = with_prev(to_blocks(k))
    vc = with_prev(to_blocks(v))
    scores = jnp.einsum('bhrnqd,bhrnkd->bhrnqk', qb, kc)
    q_idx = band + jnp.arange(band)
    k_idx = jnp.arange(2 * band)
    dist = q_idx[:, None] - k_idx[None, :]
    in_band = (dist >= 0) & (dist <= band)
    has_prev = (jnp.arange(n_blk) > 0)[:, None, None] | (k_idx >= band)[None, None, :]
    mask = in_band[None] & has_prev
    scores = jnp.where(mask, scores, -jnp.inf)
    m = scores.max(axis=-1)
    p = jnp.exp(scores - m[..., None])
    l = p.sum(axis=-1)
    o = jnp.einsum('bhrnqk,bhrnkd->bhrnqd', p, vc)

    def from_blocks(t):
        t = t.reshape(b, h, dilation, length, -1).transpose(0, 1, 3, 2, 4).reshape(b, h, s_pad, -1)
        return t[:, :, :s]

    return from_blocks(o), from_blocks(m[..., None])[..., 0], from_blocks(l[..., None])[..., 0]


def _dilated_attention(q, k, v):
    outs = [_dilated_window_branch(q, k, v, w, d) for (w, d) in DILATED_PATTERNS]
    m_all = jnp.maximum(jnp.maximum(outs[0][1], outs[1][1]), outs[2][1])
    num = jnp.zeros_like(q)
    den = jnp.zeros_like(m_all)
    for o_i, m_i, l_i in outs:
        a_i = jnp.exp(m_i - m_all)
        num = num + a_i[..., None] * o_i
        den = den + a_i * l_i
    return num / den[..., None]


def _mlstm(q, k, v, i_pre, log_f):
    b, h, s, dk = q.shape
    dv = v.shape[-1]
    nc = s // CHUNK

    def chunks(t):
        return jnp.moveaxis(t.reshape((b, h, nc, CHUNK) + t.shape[3:]), 2, 0)

    causal = jnp.tril(jnp.ones((CHUNK, CHUNK), dtype=bool))

    def step(carry, inp):
        c_st, n_st, m_st = carry
        qc, kc, vc, ic, fc = inp
        bcum = jnp.cumsum(fc, axis=-1)
        d_log = bcum[..., :, None] - bcum[..., None, :] + ic[..., None, :]
        d_log = jnp.where(causal, d_log, -jnp.inf)
        m_inter = bcum + m_st[..., None]
        m_comb = jnp.maximum(m_inter, d_log.max(axis=-1))
        w_intra = jnp.exp(d_log - m_comb[..., None])
        w_inter = jnp.exp(m_inter - m_comb)
        qk = jnp.einsum('bhtd,bhsd->bhts', qc, kc) * w_intra
        num = jnp.einsum('bhts,bhsv->bhtv', qk, vc) + w_inter[..., None] * jnp.einsum('bhtd,bhdv->bhtv', qc, c_st)
        den = qk.sum(axis=-1) + w_inter * jnp.einsum('bhtd,bhd->bht', qc, n_st)
        h_out = num / jnp.maximum(jnp.abs(den), jnp.exp(-m_comb))[..., None]
        b_last = bcum[..., -1]
        decay_s = b_last[..., None] - bcum + ic
        m_new = jnp.maximum(b_last + m_st, decay_s.max(axis=-1))
        w_s = jnp.exp(decay_s - m_new[..., None])
        w_c = jnp.exp(b_last + m_st - m_new)
        c_new = w_c[..., None, None] * c_st + jnp.einsum('bhs,bhsd,bhsv->bhdv', w_s, kc, vc)
        n_new = w_c[..., None] * n_st + jnp.einsum('bhs,bhsd->bhd', w_s, kc)
        return (c_new, n_new, m_new), h_out

    init = (jnp.zeros((b, h, dk, dv), jnp.float32), jnp.zeros((b, h, dk), jnp.float32), jnp.zeros((b, h), jnp.float32))
    xs = (chunks(q), chunks(k), chunks(v), chunks(i_pre), chunks(log_f))
    _, hs = lax.scan(step, init, xs)
    return jnp.moveaxis(hs, 0, 2).reshape(b, h, s, dv)


def _mixer(xn, w_in, b_igate, b_fgate, conv_w, g_q, g_k, w_out):
    b, s, _ = xn.shape
    proj = xn @ w_in
    qa, ka, va, qm, km, vm, om, ig, fg = jnp.split(proj, SPLIT_POINTS, axis=-1)
    f32 = jnp.float32

    def heads(t, nh, hd):
        return t.astype(f32).reshape(b, s, nh, hd).transpose(0, 2, 1, 3)
    qa = _rmsnorm(heads(qa, N_HEADS_ATT, HD_ATT), g_q) * (HD_ATT ** -0.5)
    ka = _rmsnorm(heads(ka, N_HEADS_ATT, HD_ATT), g_k)
    va = heads(va, N_HEADS_ATT, HD_ATT)
    y_att = _dilated_attention(qa, ka, va).transpose(0, 2, 1, 3).reshape(b, s, D_ATT)

    qk_in = jnp.concatenate([qm, km], axis=-1).astype(f32)
    qk_conv = lax.conv_general_dilated(
        qk_in, conv_w.astype(f32)[:, None, :], window_strides=(1,), padding=[(CONV_W - 1, 0)],
        dimension_numbers=('NWC', 'WIO', 'NWC'), feature_group_count=2 * D_ML)
    qk_conv = jax.nn.silu(qk_conv)
    qm_c, km_c = qk_conv[..., :D_ML], qk_conv[..., D_ML:]
    qml = heads(qm_c, N_HEADS_ML, HD_ML)
    kml = heads(km_c, N_HEADS_ML, HD_ML) * (HD_ML ** -0.5)
    vml = heads(vm, N_HEADS_ML, HD_ML)
    i_pre = (ig.astype(f32) + b_igate.astype(f32)).transpose(0, 2, 1)
    log_f = jax.nn.log_sigmoid(fg.astype(f32) + b_fgate.astype(f32)).transpose(0, 2, 1)
    h_ml = _mlstm(qml, kml, vml, i_pre, log_f).transpose(0, 2, 1, 3).reshape(b, s, D_ML)
    y_ml = jax.nn.sigmoid(om.astype(f32)) * h_ml

    y = jnp.concatenate([y_att, y_ml], axis=-1).astype(xn.dtype)
    return y @ w_out


def _hier_moe(xn, w_group, b_group, w_expert_router, b_expert_router, w1, w3, w2):
    b, s, d = xn.shape
    t = b * s
    xf = xn.reshape(t, d)
    g_logits = (xf @ w_group).astype(jnp.float32) + b_group.astype(jnp.float32)
    g_prob = jax.nn.softmax(g_logits, axis=-1)
    g_sel = jnp.argmax(g_logits, axis=-1)
    g_gate = jnp.take_along_axis(g_prob, g_sel[:, None], axis=-1)
    e_logits = ((xf @ w_expert_router).astype(jnp.float32) + b_expert_router.astype(jnp.float32))
    e_logits = e_logits.reshape(t, N_GROUPS, EXPERTS_PER_GROUP)
    e_logits = jnp.take_along_axis(e_logits, g_sel[:, None, None], axis=1)[:, 0]
    top_vals, top_idx = lax.top_k(e_logits, TOP_K)
    gates = g_gate * jax.nn.softmax(top_vals, axis=-1)

    n_assign = t * TOP_K
    expert_id = (g_sel[:, None] * EXPERTS_PER_GROUP + top_idx).reshape(-1)
    token_id = jnp.repeat(jnp.arange(t, dtype=jnp.int32), TOP_K)
    gate_flat = gates.reshape(-1)
    order = jnp.argsort(expert_id)
    e_sorted = expert_id[order]
    tok_sorted = token_id[order]
    gate_sorted = gate_flat[order]
    counts = jnp.bincount(expert_id, length=N_EXPERTS)
    padded = (counts + MOE_BLOCK - 1) // MOE_BLOCK * MOE_BLOCK
    start = jnp.cumsum(counts) - counts
    pend = jnp.cumsum(padded)
    pstart = pend - padded
    dest = pstart[e_sorted] + (jnp.arange(n_assign) - start[e_sorted])
    n_blk = -(-n_assign // MOE_BLOCK) + N_EXPERTS
    cap = n_blk * MOE_BLOCK
    buf_tok = jnp.zeros((cap,), jnp.int32).at[dest].set(tok_sorted)
    buf_gate = jnp.zeros((cap,), jnp.float32).at[dest].set(gate_sorted)
    blk_expert = jnp.minimum(jnp.searchsorted(pend, jnp.arange(n_blk) * MOE_BLOCK, side='right'), N_EXPERTS - 1)
    xb = xf[buf_tok].reshape(n_blk, MOE_BLOCK, d)

    def expert_block(args):
        xblk, e = args
        hdn = jax.nn.silu(xblk @ w1[e]) * (xblk @ w3[e])
        return hdn @ w2[e]

    yb = lax.map(expert_block, (xb, blk_expert)).reshape(cap, d)
    y = jnp.zeros_like(xf).at[buf_tok].add(yb * buf_gate[:, None].astype(yb.dtype))
    return y.reshape(b, s, d)


def setup_inputs(seed: int = 0) -> dict:
    key = jax.random.key(seed)
    ks = jax.random.split(key, 20)
    f32 = jnp.float32
    nrm = lambda k, shape, scale: jax.random.normal(k, shape, f32) * scale
    x = nrm(ks[0], (BATCH, SEQ, D_MODEL), 1.0)
    g_norm_mix = 1.0 + nrm(ks[1], (DEPTH, D_MODEL), 0.05)
    w_in = nrm(ks[2], (DEPTH, D_MODEL, D_IN), D_MODEL ** -0.5)
    b_igate = nrm(ks[3], (DEPTH, N_HEADS_ML), 0.1)
    b_fgate = jnp.linspace(3.0, 6.0, N_HEADS_ML, dtype=f32)[None, :] + nrm(ks[4], (DEPTH, N_HEADS_ML), 0.1)
    conv_w = nrm(ks[5], (DEPTH, CONV_W, 2 * D_ML), CONV_W ** -0.5)
    g_q = 1.0 + nrm(ks[6], (DEPTH, HD_ATT), 0.05)
    g_k = 1.0 + nrm(ks[7], (DEPTH, HD_ATT), 0.05)
    w_out = nrm(ks[8], (DEPTH, D_MIX, D_MODEL), D_MIX ** -0.5 / np.sqrt(2.0 * DEPTH))
    g_norm_ffn = 1.0 + nrm(ks[9], (DEPTH, D_MODEL), 0.05)
    w_group = nrm(ks[10], (DEPTH, D_MODEL, N_GROUPS), D_MODEL ** -0.5)
    b_group = nrm(ks[11], (DEPTH, N_GROUPS), 0.01)
    w_expert_router = nrm(ks[12], (DEPTH, D_MODEL, N_EXPERTS), D_MODEL ** -0.5)
    b_expert_router = nrm(ks[13], (DEPTH, N_EXPERTS), 0.01)
    w1 = nrm(ks[14], (DEPTH, N_EXPERTS, D_MODEL, D_EXPERT), D_MODEL ** -0.5)
    w3 = nrm(ks[15], (DEPTH, N_EXPERTS, D_MODEL, D_EXPERT), D_MODEL ** -0.5)
    w2 = nrm(ks[16], (DEPTH, N_EXPERTS, D_EXPERT, D_MODEL), D_EXPERT ** -0.5 / np.sqrt(2.0 * DEPTH))
    return {'x': x, 'g_norm_mix': g_norm_mix, 'w_in': w_in, 'b_igate': b_igate, 'b_fgate': b_fgate,
            'conv_w': conv_w, 'g_q': g_q, 'g_k': g_k, 'w_out': w_out, 'g_norm_ffn': g_norm_ffn,
            'w_group': w_group, 'b_group': b_group, 'w_expert_router': w_expert_router,
            'b_expert_router': b_expert_router, 'w1': w1, 'w3': w3, 'w2': w2}


def reference(x, g_norm_mix, w_in, b_igate, b_fgate, conv_w, g_q, g_k, w_out, g_norm_ffn,
              w_group, b_group, w_expert_router, b_expert_router, w1, w3, w2):
    for layer in range(DEPTH):
        h = _rmsnorm(x, g_norm_mix[layer])
        x = x + _mixer(h, w_in[layer], b_igate[layer], b_fgate[layer], conv_w[layer],
                       g_q[layer], g_k[layer], w_out[layer])
        h = _rmsnorm(x, g_norm_ffn[layer])
        x = x + _hier_moe(h, w_group[layer], b_group[layer], w_expert_router[layer], b_expert_router[layer],
                          w1[layer], w3[layer], w2[layer])
    return x
```

```python
import functools

import jax
import jax.numpy as jnp
from jax import lax
from jax.experimental import pallas as pl
from jax.experimental.pallas import tpu as pltpu

F32 = jnp.float32
BF16 = jnp.bfloat16

EPS = 1e-6
D_MODEL = 1024
N_HEADS_ATT = 8
HD_ATT = 64
D_ATT = N_HEADS_ATT * HD_ATT
N_PAIRS = D_ATT // 128
DILATIONS = (1, 4, 16)
BAND = 128
ATT_TILE = BAND * 16
N_HEADS_ML = 4
HD_ML = 128
D_ML = N_HEADS_ML * HD_ML
CONV_W = 4
CONV_HALO = 16
ML_CHUNK = 256
N_GROUPS = 4
EXPERTS_PER_GROUP = 8
N_EXPERTS = N_GROUPS * EXPERTS_PER_GROUP
TOP_K = 2
D_EXPERT = 512
MOE_BLOCK = 256
LANES = 128
NEG = -1e30
VMEM_LIMIT = 56 * 1024 * 1024


def _cparams(sem):
    return pltpu.CompilerParams(dimension_semantics=sem, vmem_limit_bytes=VMEM_LIMIT)


def _inproj_body(x_ref, g_ref, watt_ref, wml_ref, wg_ref, wgt_ref, bd_ref, gq_ref, gk_ref,
                 q_ref, k_ref, v_ref, ml_ref, gcol_ref, grow_ref):
    x = x_ref[...]
    ms = jnp.mean(x * x, axis=-1, keepdims=True)
    xn = (x * lax.rsqrt(ms + EPS) * g_ref[...]).astype(BF16)
    att = jnp.dot(xn, watt_ref[...], preferred_element_type=F32)

    def headnorm(t, gain):
        msq = jnp.dot((t * t).astype(BF16), bd_ref[...], preferred_element_type=F32)
        return t * lax.rsqrt(msq + EPS) * gain

    q = headnorm(att[:, :D_ATT], gq_ref[...])
    k = headnorm(att[:, D_ATT:2 * D_ATT], gk_ref[...])
    v = att[:, 2 * D_ATT:]
    for p in range(N_PAIRS):
        sl = slice(LANES * p, LANES * (p + 1))
        q_ref[p] = q[:, sl].astype(BF16)
        k_ref[p] = k[:, sl].astype(BF16)
        v_ref[p] = v[:, sl].astype(BF16)
    ml_ref[...] = jnp.dot(xn, wml_ref[...], preferred_element_type=F32).astype(BF16)
    gcol_ref[...] = jnp.dot(xn, wg_ref[...], preferred_element_type=F32)
    grow_ref[...] = lax.dot_general(wgt_ref[...], xn, (((1,), (1,)), ((), ())),
                                    preferred_element_type=F32)


def _inproj(x, g, watt, wml, wg, wgt, bd, gq, gk, layer, tm=512):
    b, s, d = x.shape
    ns = s // tm
    xf = x.reshape(b * s, d)
    pair_spec = pl.BlockSpec((None, N_PAIRS, tm, LANES), lambda i: (i // ns, 0, i % ns, 0))
    pair_shape = jax.ShapeDtypeStruct((b, N_PAIRS, s, LANES), BF16)
    lw = lambda shape: pl.BlockSpec((None,) + shape, lambda i: (layer,) + (0,) * len(shape))
    return pl.pallas_call(
        _inproj_body,
        grid=(b * ns,),
        in_specs=[
            pl.BlockSpec((tm, d), lambda i: (i, 0)),
            lw((1, d)), lw((d, 3 * D_ATT)), lw((d, 4 * D_ML)), lw((d, 2 * N_HEADS_ML)),
            lw((2 * N_HEADS_ML, d)),
            pl.BlockSpec((D_ATT, D_ATT), lambda i: (0, 0)),
            lw((1, D_ATT)), lw((1, D_ATT)),
        ],
        out_specs=[
            pair_spec, pair_spec, pair_spec,
            pl.BlockSpec((tm, 4 * D_ML), lambda i: (i, 0)),
            pl.BlockSpec((tm, 2 * N_HEADS_ML), lambda i: (i, 0)),
            pl.BlockSpec((None, 2 * N_HEADS_ML, tm), lambda i: (i // ns, 0, i % ns)),
        ],
        out_shape=[
            pair_shape, pair_shape, pair_shape,
            jax.ShapeDtypeStruct((b * s, 4 * D_ML), BF16),
            jax.ShapeDtypeStruct((b * s, 2 * N_HEADS_ML), F32),
            jax.ShapeDtypeStruct((b, 2 * N_HEADS_ML, s), F32),
        ],
        compiler_params=_cparams(("parallel",)),
        name="inproj",
    )(xf, g, watt, wml, wg, wgt, bd, gq, gk)


def _attn_body(q1, k1, v1, kp1, vp1, q4, k4, v4, kp4, vp4, q16, k16, v16, kp16, vp16,
               o_ref, acc_ref, m_ref, l_ref):
    tile = pl.program_id(2)
    lane = lax.broadcasted_iota(jnp.int32, (1, LANES), 1)
    head_mask = (lane < HD_ATT, lane >= HD_ATT)
    qi = lax.broadcasted_iota(jnp.int32, (BAND, 2 * BAND), 0)
    kj = lax.broadcasted_iota(jnp.int32, (BAND, 2 * BAND), 1)
    band = (kj >= qi) & (kj <= qi + BAND)
    bias_prev = jnp.where(band, 0.0, NEG).astype(F32)
    bias_noprev = jnp.where(band & (kj >= BAND), 0.0, NEG).astype(F32)

    def unit(q, kcat, vcat, has_prev, rows, mode):
        bias = jnp.where(has_prev, bias_prev, bias_noprev)
        new = []
        for h in range(2):
            qh = jnp.where(head_mask[h], q, jnp.zeros_like(q))
            s = lax.dot_general(qh, kcat, (((1,), (1,)), ((), ())), preferred_element_type=F32) + bias
            mrow = jnp.max(s, axis=-1, keepdims=True)
            if mode == "init":
                m_new = mrow
            else:
                m_old = m_ref[h, rows, :][:, 0:1]
                l_old = l_ref[h, rows, :][:, 0:1]
                m_new = jnp.maximum(m_old, mrow)
                alpha = jnp.exp(m_old - m_new)
            p = jnp.exp(s - m_new)
            l_new = jnp.sum(p, axis=-1, keepdims=True)
            pv = jnp.dot(p.astype(BF16), vcat, preferred_element_type=F32)
            if mode != "init":
                l_new = l_new + alpha * l_old
            else:
                alpha = None
            if mode != "final":
                m_ref[h, rows, :] = jnp.broadcast_to(m_new, (BAND, LANES))
                l_ref[h, rows, :] = jnp.broadcast_to(l_new, (BAND, LANES))
            new.append((pv, alpha, l_new))
        (pv0, a0, l0), (pv1, a1, l1) = new
        if mode == "init":
            acc = jnp.where(head_mask[0], pv0, pv1)
        else:
            old = acc_ref[rows, :]
            acc = jnp.where(head_mask[0], a0 * old + pv0, a1 * old + pv1)
        if mode == "final":
            o_ref[rows, :] = (acc / jnp.where(head_mask[0], l0, l1)).astype(o_ref.dtype)
        else:
            acc_ref[rows, :] = acc

    d = 16
    for r in range(d):
        ln = slice(r * LANES, (r + 1) * LANES)
        kcat = jnp.concatenate([kp16[:, ln], k16[:, ln]], axis=0)
        vcat = jnp.concatenate([vp16[:, ln], v16[:, ln]], axis=0)
        unit(q16[:, ln], kcat, vcat, tile > 0, pl.ds(r, BAND, stride=d), "init")

    d = 4
    for r in range(d):
        ln = slice(r * LANES, (r + 1) * LANES)

        def blk4(n, c, ln=ln, r=r):
            cur = pl.ds(pl.multiple_of(n * BAND, BAND), BAND)
            prv = pl.ds(pl.multiple_of(jnp.maximum(n - 1, 0) * BAND, BAND), BAND)
            kp = jnp.where(n > 0, k4[prv, ln], kp4[:, ln])
            vp = jnp.where(n > 0, v4[prv, ln], vp4[:, ln])
            kcat = jnp.concatenate([kp, k4[cur, ln]], axis=0)
            vcat = jnp.concatenate([vp, v4[cur, ln]], axis=0)
            unit(q4[cur, ln], kcat, vcat, (tile > 0) | (n > 0),
                 pl.ds(n * (BAND * 4) + r, BAND, stride=4), "update")
            return c
        lax.fori_loop(0, ATT_TILE // (BAND * d), blk4, 0)

    def blk1(n, c):
        cur = pl.ds(pl.multiple_of(n * BAND, BAND), BAND)
        prv = pl.ds(pl.multiple_of(jnp.maximum(n - 1, 0) * BAND, BAND), BAND)
        kp = jnp.where(n > 0, k1[prv, :], kp1[...])
        vp = jnp.where(n > 0, v1[prv, :], vp1[...])
        kcat = jnp.concatenate([kp, k1[cur, :]], axis=0)
        vcat = jnp.concatenate([vp, v1[cur, :]], axis=0)
        unit(q1[cur, :], kcat, vcat, (tile > 0) | (n > 0), cur, "final")
        return c
    lax.fori_loop(0, ATT_TILE // BAND, blk1, 0)


def _attention(q, k, v):
    b, npair, s, _ = q.shape
    nt = s // ATT_TILE
    in_specs, args = [], []
    for d in DILATIONS:
        rows = ATT_TILE // d
        shape = (b, npair, s // d, d * LANES)
        cur = pl.BlockSpec((None, None, rows, d * LANES), lambda bi, p, i: (bi, p, i, 0))
        nb = rows // BAND
        prev = pl.BlockSpec((None, None, BAND, d * LANES),
                            lambda bi, p, i, nb=nb: (bi, p, jnp.maximum(i * nb - 1, 0), 0))
        qv, kv, vv = (t.reshape(shape) for t in (q, k, v))
        in_specs += [cur, cur, cur, prev, prev]
        args += [qv, kv, vv, kv, vv]
    return pl.pallas_call(
        _attn_body,
        grid=(b, npair, nt),
        in_specs=in_specs,
        out_specs=pl.BlockSpec((None, None, ATT_TILE, LANES), lambda bi, p, i: (bi, p, i, 0)),
        out_shape=jax.ShapeDtypeStruct((b, npair, s, LANES), BF16),
        scratch_shapes=[
            pltpu.VMEM((ATT_TILE, LANES), F32),
            pltpu.VMEM((2, ATT_TILE, LANES), F32),
            pltpu.VMEM((2, ATT_TILE, LANES), F32),
        ],
        compiler_params=_cparams(("parallel", "parallel", "parallel")),
        name="dilated_attn",
    )(*args)


def _log_sigmoid(z):
    return jnp.minimum(z, 0.0) - jnp.log(1.0 + jnp.exp(-jnp.abs(z)))


def _mlstm_body(qk_ref, v_ref, o_ref, gcol_ref, grow_ref, convw_ref, bcol_ref, brow_ref,
                y_ref, xc_ref, c_ref, n_ref, m_ref):
    L = ML_CHUNK
    chunk = pl.program_id(1)

    @pl.when(chunk == 0)
    def _():
        xc_ref[0:CONV_HALO, :] = jnp.zeros((CONV_HALO, 2 * D_ML), F32)
        c_ref[...] = jnp.zeros_like(c_ref)
        n_ref[...] = jnp.zeros_like(n_ref)
        m_ref[...] = jnp.zeros_like(m_ref)

    xc_ref[CONV_HALO:CONV_HALO + L, :] = qk_ref[...].astype(F32)
    w = convw_ref[...]
    conv = jnp.zeros((L, 2 * D_ML), F32)
    for j in range(CONV_W):
        off = CONV_HALO - (CONV_W - 1) + j
        conv = conv + w[j:j + 1, :] * xc_ref[off:off + L, :]
    xc_ref[0:CONV_HALO, :] = xc_ref[L:L + CONV_HALO, :]
    qkc = conv * jax.nn.sigmoid(conv)

    gc = gcol_ref[...] + bcol_ref[...]
    gr = grow_ref[...] + brow_ref[...]
    i_col, f_col = gc[:, :N_HEADS_ML], _log_sigmoid(gc[:, N_HEADS_ML:])
    i_row, f_row = gr[:N_HEADS_ML, :], _log_sigmoid(gr[N_HEADS_ML:, :])
    t_idx = lax.broadcasted_iota(jnp.int32, (L, L), 0)
    s_idx = lax.broadcasted_iota(jnp.int32, (L, L), 1)
    causal = s_idx <= t_idx

    for h in range(N_HEADS_ML):
        ln = slice(h * HD_ML, (h + 1) * HD_ML)
        q = qkc[:, ln]
        k = qkc[:, D_ML + h * HD_ML:D_ML + (h + 1) * HD_ML] * (HD_ML ** -0.5)
        v = v_ref[:, ln]
        qb, kb = q.astype(BF16), k.astype(BF16)
        b_col = jnp.sum(jnp.where(causal, f_row[h:h + 1, :], 0.0), axis=1, keepdims=True)
        b_row = jnp.sum(jnp.where(t_idx <= s_idx, f_col[:, h:h + 1], 0.0), axis=0, keepdims=True)
        a_row = i_row[h:h + 1, :] - b_row
        a_col = i_col[:, h:h + 1] - b_col
        m_st = m_ref[h][0:1, 0:1]
        a_mat = jnp.where(causal, a_row, -jnp.inf)
        m_col = jnp.maximum(jnp.max(a_mat, axis=1, keepdims=True), m_st)
        w_intra = jnp.exp(a_mat - m_col)
        w_inter = jnp.exp(m_st - m_col)
        qk = lax.dot_general(qb, kb, (((1,), (1,)), ((), ())), preferred_element_type=F32) * w_intra
        c_st = c_ref[h]
        n_st = n_ref[h][0:1, :]
        num = (jnp.dot(qk.astype(BF16), v, preferred_element_type=F32)
               + w_inter * jnp.dot(qb, c_st.astype(BF16), preferred_element_type=F32))
        den = (jnp.sum(qk, axis=1, keepdims=True)
               + w_inter * jnp.sum(q * n_st, axis=1, keepdims=True))
        h_out = num / jnp.maximum(jnp.abs(den), jnp.exp(-(b_col + m_col)))
        m_last = m_col[L - 1:L, :]
        b_last = b_col[L - 1:L, :]
        w_s = jnp.exp(a_col - m_last)
        w_c = jnp.exp(m_st - m_last)
        vs = (w_s * v.astype(F32)).astype(BF16)
        c_ref[h] = w_c * c_st + lax.dot_general(kb, vs, (((0,), (0,)), ((), ())),
                                                preferred_element_type=F32)
        n_new = w_c * n_st + jnp.sum(w_s * k, axis=0, keepdims=True)
        n_ref[h] = jnp.broadcast_to(n_new, (8, HD_ML))
        m_ref[h] = jnp.broadcast_to(b_last + m_last, (8, LANES))
        y_ref[:, ln] = (jax.nn.sigmoid(o_ref[:, ln].astype(F32)) * h_out).astype(y_ref.dtype)


def _mlstm(ml, gcol, grow, convw, bcol, brow, layer, b, s):
    L = ML_CHUNK
    nc = s // L
    lw = lambda shape: pl.BlockSpec((None,) + shape, lambda bi, i: (layer,) + (0,) * len(shape))
    return pl.pallas_call(
        _mlstm_body,
        grid=(b, nc),
        in_specs=[
            pl.BlockSpec((L, 2 * D_ML), lambda bi, i: (bi * nc + i, 0)),
            pl.BlockSpec((L, D_ML), lambda bi, i: (bi * nc + i, 2)),
            pl.BlockSpec((L, D_ML), lambda bi, i: (bi * nc + i, 3)),
            pl.BlockSpec((L, 2 * N_HEADS_ML), lambda bi, i: (bi * nc + i, 0)),
            pl.BlockSpec((None, 2 * N_HEADS_ML, L), lambda bi, i: (bi, 0, i)),
            lw((CONV_W, 2 * D_ML)), lw((1, 2 * N_HEADS_ML)), lw((2 * N_HEADS_ML, 1)),
        ],
        out_specs=pl.BlockSpec((L, D_ML), lambda bi, i: (bi * nc + i, 0)),
        out_shape=jax.ShapeDtypeStruct((b * s, D_ML), BF16),
        scratch_shapes=[
            pltpu.VMEM((L + CONV_HALO, 2 * D_ML), F32),
            pltpu.VMEM((N_HEADS_ML, HD_ML, HD_ML), F32),
            pltpu.VMEM((N_HEADS_ML, 8, HD_ML), F32),
            pltpu.VMEM((N_HEADS_ML, 8, LANES), F32),
        ],
        compiler_params=_cparams(("parallel", "arbitrary")),
        name="mlstm",
    )(ml, ml, ml, gcol, grow, convw, bcol, brow)


def _outproj_body(x_ref, ya_ref, yml_ref, wout_ref, g_ref, wr_ref, br_ref,
                  xnew_ref, hn_ref, ri_ref, rg_ref, cnt_ref):
    tm = x_ref.shape[0]

    @pl.when(pl.program_id(0) == 0)
    def _():
        cnt_ref[...] = jnp.zeros_like(cnt_ref)

    y = jnp.concatenate([ya_ref[p] for p in range(N_PAIRS)] + [yml_ref[...]], axis=-1)
    xnew = x_ref[...] + jnp.dot(y, wout_ref[...], preferred_element_type=F32)
    xnew_ref[...] = xnew
    ms = jnp.mean(xnew * xnew, axis=-1, keepdims=True)
    hn = xnew * lax.rsqrt(ms + EPS) * g_ref[...]
    hn_ref[...] = hn.astype(hn_ref.dtype)

    logits = jnp.dot(hn, wr_ref[...], preferred_element_type=F32,
                     precision=lax.Precision.HIGHEST) + br_ref[...]
    lane = lax.broadcasted_iota(jnp.int32, (tm, LANES), 1)
    lane_f = lane.astype(F32)
    big = float(LANES)

    def first_max(vals, valid):
        masked = jnp.where(valid, vals, -jnp.inf)
        top = jnp.max(masked, axis=-1, keepdims=True)
        idx = jnp.min(jnp.where(valid & (masked == top), lane_f, big), axis=-1, keepdims=True)
        return top, idx

    is_group = lane < N_GROUPS
    g_top, g_sel = first_max(logits, is_group)
    g_gate = 1.0 / jnp.sum(jnp.where(is_group, jnp.exp(logits - g_top), 0.0), axis=-1, keepdims=True)
    lo = N_GROUPS + EXPERTS_PER_GROUP * g_sel
    in_group = (lane_f >= lo) & (lane_f < lo + EXPERTS_PER_GROUP)
    t1, i1 = first_max(logits, in_group)
    t2, i2 = first_max(logits, in_group & (lane_f != i1))
    r = jnp.exp(t2 - t1)
    p1 = 1.0 / (1.0 + r)
    e1 = i1 - N_GROUPS
    e2 = i2 - N_GROUPS

    hit1 = lane_f == e1
    hit2 = lane_f == e2
    onehot = jnp.where(hit1 | hit2, 1.0, 0.0)
    ti = lax.broadcasted_iota(jnp.int32, (tm, tm), 0)
    tj = lax.broadcasted_iota(jnp.int32, (tm, tm), 1)
    strict_lower = jnp.where(tj < ti, 1.0, 0.0).astype(BF16)
    before = jnp.dot(strict_lower, onehot.astype(BF16), preferred_element_type=F32) + cnt_ref[...]
    rank1 = jnp.sum(jnp.where(hit1, before, 0.0), axis=-1, keepdims=True)
    rank2 = jnp.sum(jnp.where(hit2, before, 0.0), axis=-1, keepdims=True)
    cnt_ref[...] = cnt_ref[...] + jnp.sum(onehot, axis=0, keepdims=True)

    ri = jnp.where(lane == 0, e1, jnp.where(lane == 1, e2, jnp.where(lane == 2, rank1,
                   jnp.where(lane == 3, rank2, 0.0))))
    ri_ref[...] = ri.astype(jnp.int32)
    rg_ref[...] = jnp.where(lane == 0, g_gate * p1, jnp.where(lane == 1, g_gate * r * p1, 0.0))


def _outproj_router(x, ya, yml, wout, g, wr, br, layer, tm=512):
    b, s, d = x.shape
    ns = s // tm
    t = b * s
    xf = x.reshape(t, d)
    lw = lambda shape: pl.BlockSpec((None,) + shape, lambda i: (layer,) + (0,) * len(shape))
    row = lambda width: pl.BlockSpec((tm, width), lambda i: (i, 0))
    return pl.pallas_call(
        _outproj_body,
        grid=(t // tm,),
        in_specs=[
            row(d),
            pl.BlockSpec((None, N_PAIRS, tm, LANES), lambda i: (i // ns, 0, i % ns, 0)),
            row(D_ML),
            lw((d, d)), lw((1, d)), lw((d, LANES)), lw((1, LANES)),
        ],
        out_specs=[row(d), row(d), row(LANES), row(LANES), pl.BlockSpec((1, LANES), lambda i: (0, 0))],
        out_shape=[
            jax.ShapeDtypeStruct((t, d), F32),
            jax.ShapeDtypeStruct((t, d), BF16),
            jax.ShapeDtypeStruct((t, LANES), jnp.int32),
            jax.ShapeDtypeStruct((t, LANES), F32),
            jax.ShapeDtypeStruct((1, LANES), F32),
        ],
        compiler_params=_cparams(("arbitrary",)),
        name="outproj_router",
    )(xf, ya, yml, wout, g, wr, br)


def _expert_body(be_ref, nu_ref, xs_ref, w1_ref, w3_ref, w2_ref, yb_ref, w1b, w3b, w2b):
    i = pl.program_id(0)
    e = be_ref[i]
    e_prev = be_ref[jnp.maximum(i - 1, 0)]

    @pl.when((i == 0) | (e != e_prev))
    def _():
        w1b[...] = w1_ref[...].astype(BF16)
        w3b[...] = w3_ref[...].astype(BF16)
        w2b[...] = w2_ref[...].astype(BF16)

    @pl.when(i < nu_ref[0])
    def _():
        x = xs_ref[...]
        a = jnp.dot(x, w1b[...], preferred_element_type=F32)
        g = jnp.dot(x, w3b[...], preferred_element_type=F32)
        hdn = (a * jax.nn.sigmoid(a) * g).astype(BF16)
        yb_ref[...] = jnp.dot(hdn, w2b[...], preferred_element_type=F32).astype(yb_ref.dtype)

    @pl.when(i >= nu_ref[0])
    def _():
        yb_ref[...] = jnp.zeros_like(yb_ref)


def _experts(xs, blk_expert, n_used, w1, w3, w2, layer):
    cap, d = xs.shape
    n_blk = cap // MOE_BLOCK
    grid_spec = pltpu.PrefetchScalarGridSpec(
        num_scalar_prefetch=2,
        grid=(n_blk,),
        in_specs=[
            pl.BlockSpec((MOE_BLOCK, d), lambda i, be, nu: (i, 0)),
            pl.BlockSpec((None, None, d, D_EXPERT), lambda i, be, nu: (layer, be[i], 0, 0)),
            pl.BlockSpec((None, None, d, D_EXPERT), lambda i, be, nu: (layer, be[i], 0, 0)),
            pl.BlockSpec((None, None, D_EXPERT, d), lambda i, be, nu: (layer, be[i], 0, 0)),
        ],
        out_specs=pl.BlockSpec((MOE_BLOCK, d), lambda i, be, nu: (i, 0)),
        scratch_shapes=[
            pltpu.VMEM((d, D_EXPERT), BF16),
            pltpu.VMEM((d, D_EXPERT), BF16),
            pltpu.VMEM((D_EXPERT, d), BF16),
        ],
    )
    return pl.pallas_call(
        _expert_body,
        grid_spec=grid_spec,
        out_shape=jax.ShapeDtypeStruct((cap, d), BF16),
        compiler_params=_cparams(("arbitrary",)),
        name="experts",
    )(blk_expert, n_used, xs, w1, w3, w2)


def _combine_body(x_ref, y0_ref, y1_ref, rg_ref, o_ref):
    g = rg_ref[...]
    o_ref[...] = (x_ref[...] + g[:, 0:1] * y0_ref[...].astype(F32)
                  + g[:, 1:2] * y1_ref[...].astype(F32))


def _combine(xnew, y0, y1, rg, tm=1024):
    t, d = xnew.shape
    row = lambda width: pl.BlockSpec((tm, width), lambda i: (i, 0))
    return pl.pallas_call(
        _combine_body,
        grid=(t // tm,),
        in_specs=[row(d), row(d), row(d), row(LANES)],
        out_specs=row(d),
        out_shape=jax.ShapeDtypeStruct((t, d), F32),
        compiler_params=_cparams(("parallel",)),
        name="combine",
    )(xnew, y0, y1, rg)


def _moe_dispatch_plan(ri, cnt, t):
    n_blk = (t * TOP_K) // MOE_BLOCK + N_EXPERTS
    counts = cnt[0, :N_EXPERTS].astype(jnp.int32)
    padded = (counts + MOE_BLOCK - 1) // MOE_BLOCK * MOE_BLOCK
    pend = jnp.cumsum(padded)
    pstart = pend - padded
    dest = pstart[ri[:, 0:2]] + ri[:, 2:4]
    blk_expert = jnp.minimum(
        jnp.searchsorted(pend, jnp.arange(n_blk, dtype=jnp.int32) * MOE_BLOCK, side="right"),
        N_EXPERTS - 1).astype(jnp.int32)
    n_used = (pend[-1:] // MOE_BLOCK).astype(jnp.int32)
    return dest, blk_expert, n_used, n_blk


def _layer(x, layer, p):
    b, s, d = x.shape
    t = b * s
    q, k, v, ml, gcol, grow = _inproj(x, p["g_mix"], p["w_att"], p["w_ml"], p["w_g"], p["w_gt"],
                                      p["bd"], p["gq"], p["gk"], layer)
    ya = _attention(q, k, v)
    yml = _mlstm(ml, gcol, grow, p["conv_w"], p["b_col"], p["b_row"], layer, b, s)
    xnew, hn, ri, rg, cnt = _outproj_router(x, ya, yml, p["w_out"], p["g_ffn"], p["w_r"], p["b_r"], layer)
    dest, blk_expert, n_used, n_blk = _moe_dispatch_plan(ri, cnt, t)
    tok = jnp.broadcast_to(jnp.arange(t, dtype=jnp.int32)[:, None], (t, TOP_K))
    buf_tok = jnp.zeros((n_blk * MOE_BLOCK,), jnp.int32).at[dest.reshape(-1)].set(tok.reshape(-1))
    xs = hn[buf_tok]
    yb = _experts(xs, blk_expert, n_used, p["w1"], p["w3"], p["w2"], layer)
    out = _combine(xnew, yb[dest[:, 0]], yb[dest[:, 1]], rg)
    return out.reshape(b, s, d)


def _prep(g_norm_mix, w_in, b_igate, b_fgate, conv_w, g_q, g_k, w_out, g_norm_ffn,
          w_group, b_group, w_expert_router, b_expert_router, w1, w3, w2):
    w_in_b = w_in.astype(BF16)
    n_gate = 2 * N_HEADS_ML
    hd_id = jnp.arange(D_ATT, dtype=jnp.int32) // HD_ATT
    pad_r = LANES - N_GROUPS - N_EXPERTS
    p = {
        "g_mix": g_norm_mix[:, None, :],
        "w_att": w_in_b[:, :, :3 * D_ATT],
        "w_ml": w_in_b[:, :, 3 * D_ATT:3 * D_ATT + 4 * D_ML],
        "w_g": w_in_b[:, :, -n_gate:],
        "w_gt": jnp.swapaxes(w_in_b[:, :, -n_gate:], 1, 2),
        "bd": jnp.where(hd_id[:, None] == hd_id[None, :], 1.0 / HD_ATT, 0.0).astype(BF16),
        "gq": (jnp.tile(g_q, (1, N_HEADS_ATT)) * (HD_ATT ** -0.5))[:, None, :],
        "gk": jnp.tile(g_k, (1, N_HEADS_ATT))[:, None, :],
        "conv_w": conv_w,
        "b_col": jnp.concatenate([b_igate, b_fgate], axis=-1)[:, None, :],
        "b_row": jnp.concatenate([b_igate, b_fgate], axis=-1)[:, :, None],
        "w_out": w_out.astype(BF16),
        "g_ffn": g_norm_ffn[:, None, :],
        "w_r": jnp.pad(jnp.concatenate([w_group, w_expert_router], axis=-1), ((0, 0), (0, 0), (0, pad_r))),
        "b_r": jnp.pad(jnp.concatenate([b_group, b_expert_router], axis=-1), ((0, 0), (0, pad_r)))[:, None, :],
        "w1": w1, "w3": w3, "w2": w2,
    }
    return p


def kernel(x, g_norm_mix, w_in, b_igate, b_fgate, conv_w, g_q, g_k, w_out, g_norm_ffn,
           w_group, b_group, w_expert_router, b_expert_router, w1, w3, w2):
    p = _prep(g_norm_mix, w_in, b_igate, b_fgate, conv_w, g_q, g_k, w_out, g_norm_ffn,
              w_group, b_group, w_expert_router, b_expert_router, w1, w3, w2)
    for layer in range(w_in.shape[0]):
        x = _layer(x, layer, p)
    return x
```

```python
import functools

import jax
import jax.numpy as jnp
from jax import lax
from jax.experimental import pallas as pl
from jax.experimental.pallas import tpu as pltpu

F32 = jnp.float32
BF16 = jnp.bfloat16

EPS = 1e-6
D_MODEL = 1024
N_HEADS_ATT = 8
HD_ATT = 64
D_ATT = N_HEADS_ATT * HD_ATT
N_PAIRS = D_ATT // 128
DILATIONS = (1, 4, 16)
BAND = 128
ATT_TILE = BAND * 16
N_HEADS_ML = 4
HD_ML = 128
D_ML = N_HEADS_ML * HD_ML
CONV_W = 4
CONV_HALO = 16
ML_CHUNK = 256
N_GROUPS = 4
EXPERTS_PER_GROUP = 8
N_EXPERTS = N_GROUPS * EXPERTS_PER_GROUP
TOP_K = 2
D_EXPERT = 512
MOE_BLOCK = 256
LANES = 128
NEG = -1e30
VMEM_LIMIT = 56 * 1024 * 1024


def _cparams(sem):
    return pltpu.CompilerParams(dimension_semantics=sem, vmem_limit_bytes=VMEM_LIMIT)


def _inproj_body(x_ref, g_ref, watt_ref, wml_ref, wg_ref, wgt_ref, bd_ref, gq_ref, gk_ref,
                 q_ref, k_ref, v_ref, ml_ref, gcol_ref, grow_ref):
    x = x_ref[...]
    ms = jnp.mean(x * x, axis=-1, keepdims=True)
    xn = (x * lax.rsqrt(ms + EPS) * g_ref[...]).astype(BF16)
    att = jnp.dot(xn, watt_ref[...], preferred_element_type=F32)

    def headnorm(t, gain):
        msq = jnp.dot((t * t).astype(BF16), bd_ref[...], preferred_element_type=F32)
        return t * lax.rsqrt(msq + EPS) * gain

    q = headnorm(att[:, :D_ATT], gq_ref[...])
    k = headnorm(att[:, D_ATT:2 * D_ATT], gk_ref[...])
    v = att[:, 2 * D_ATT:]
    for p in range(N_PAIRS):
        sl = slice(LANES * p, LANES * (p + 1))
        q_ref[p] = q[:, sl].astype(BF16)
        k_ref[p] = k[:, sl].astype(BF16)
        v_ref[p] = v[:, sl].astype(BF16)
    ml_ref[...] = jnp.dot(xn, wml_ref[...], preferred_element_type=F32).astype(BF16)
    gcol_ref[...] = jnp.dot(xn, wg_ref[...], preferred_element_type=F32)
    grow_ref[...] = lax.dot_general(wgt_ref[...], xn, (((1,), (1,)), ((), ())),
                                    preferred_element_type=F32)


def _inproj(x, g, watt, wml, wg, wgt, bd, gq, gk, layer, tm=512):
    b, s, d = x.shape
    ns = s // tm
    xf = x.reshape(b * s, d)
    pair_spec = pl.BlockSpec((None, N_PAIRS, tm, LANES), lambda i: (i // ns, 0, i % ns, 0))
    pair_shape = jax.ShapeDtypeStruct((b, N_PAIRS, s, LANES), BF16)
    lw = lambda shape: pl.BlockSpec((None,) + shape, lambda i: (layer,) + (0,) * len(shape))
    return pl.pallas_call(
        _inproj_body,
        grid=(b * ns,),
        in_specs=[
            pl.BlockSpec((tm, d), lambda i: (i, 0)),
            lw((1, d)), lw((d, 3 * D_ATT)), lw((d, 4 * D_ML)), lw((d, 2 * N_HEADS_ML)),
            lw((2 * N_HEADS_ML, d)),
            pl.BlockSpec((D_ATT, D_ATT), lambda i: (0, 0)),
            lw((1, D_ATT)), lw((1, D_ATT)),
        ],
        out_specs=[
            pair_spec, pair_spec, pair_spec,
            pl.BlockSpec((tm, 4 * D_ML), lambda i: (i, 0)),
            pl.BlockSpec((tm, 2 * N_HEADS_ML), lambda i: (i, 0)),
            pl.BlockSpec((None, 2 * N_HEADS_ML, tm), lambda i: (i // ns, 0, i % ns)),
        ],
        out_shape=[
            pair_shape, pair_shape, pair_shape,
            jax.ShapeDtypeStruct((b * s, 4 * D_ML), BF16),
            jax.ShapeDtypeStruct((b * s, 2 * N_HEADS_ML), F32),
            jax.ShapeDtypeStruct((b, 2 * N_HEADS_ML, s), F32),
        ],
        compiler_params=_cparams(("parallel",)),
        name="inproj",
    )(xf, g, watt, wml, wg, wgt, bd, gq, gk)


def _attn_body(q_ref, k_ref, v_ref, kp_ref, vp_ref, o_ref, qf, kf, vf, acc_ref, m_ref, l_ref):
    tile = pl.program_id(2)
    lane = lax.broadcasted_iota(jnp.int32, (1, LANES), 1)
    first_head = lane < HD_ATT
    qi = lax.broadcasted_iota(jnp.int32, (BAND, 2 * BAND), 0)
    kj = lax.broadcasted_iota(jnp.int32, (BAND, 2 * BAND), 1)
    band = (kj >= qi) & (kj <= qi + BAND)
    bias_prev = jnp.where(band, 0.0, NEG).astype(F32)
    bias_first = jnp.where(band & ((kj >= BAND) | (tile > 0)), 0.0, NEG).astype(F32)

    qf[...] = q_ref[...].astype(F32)
    kf[0:ATT_TILE, :] = kp_ref[...].astype(F32)
    kf[ATT_TILE:, :] = k_ref[...].astype(F32)
    vf[0:ATT_TILE, :] = vp_ref[...].astype(F32)
    vf[ATT_TILE:, :] = v_ref[...].astype(F32)

    def unit(d, r, n, mode):
        q0 = n * BAND * d + r
        k0 = ATT_TILE + (n - 1) * BAND * d + r
        rows = pl.ds(q0, BAND, stride=d) if d > 1 else pl.ds(q0, BAND)
        krows = pl.ds(k0, 2 * BAND, stride=d) if d > 1 else pl.ds(k0, 2 * BAND)
        q = qf[rows, :].astype(BF16)
        kcat = kf[krows, :].astype(BF16)
        vcat = vf[krows, :].astype(BF16)
        bias = bias_first if n == 0 else bias_prev
        if mode != "init":
            m_old = m_ref[rows, :]
            l_old = l_ref[rows, :]
        stats = []
        for h in range(2):
            qh = jnp.where(first_head if h == 0 else ~first_head, q, jnp.zeros_like(q))
            s = lax.dot_general(qh, kcat, (((1,), (1,)), ((), ())), preferred_element_type=F32) + bias
            m_new = jnp.max(s, axis=-1, keepdims=True)
            if mode != "init":
                m_new = jnp.maximum(m_new, m_old[:, h * HD_ATT:h * HD_ATT + 1])
            p = jnp.exp(s - m_new)
            l_new = jnp.sum(p, axis=-1, keepdims=True)
            pv = jnp.dot(p.astype(BF16), vcat, preferred_element_type=F32)
            stats.append((m_new, l_new, pv))
        (m0, l0, pv0), (m1, l1, pv1) = stats
        m_pair = jnp.where(first_head, m0, m1)
        l_pair = jnp.where(first_head, l0, l1)
        acc = jnp.where(first_head, pv0, pv1)
        if mode != "init":
            alpha = jnp.exp(m_old - m_pair)
            l_pair = l_pair + alpha * l_old
            acc = acc + alpha * acc_ref[rows, :]
        if mode == "final":
            o_ref[rows, :] = (acc / l_pair).astype(o_ref.dtype)
        else:
            m_ref[rows, :] = m_pair
            l_ref[rows, :] = l_pair
            acc_ref[rows, :] = acc

    for d, mode in ((16, "init"), (4, "update"), (1, "final")):
        for r in range(d):
            for n in range(ATT_TILE // (BAND * d)):
                unit(d, r, n, mode)


def _attention(q, k, v):
    b, npair, s, _ = q.shape
    cur = pl.BlockSpec((None, None, ATT_TILE, LANES), lambda bi, p, i: (bi, p, i, 0))
    prev = pl.BlockSpec((None, None, ATT_TILE, LANES), lambda bi, p, i: (bi, p, jnp.maximum(i - 1, 0), 0))
    return pl.pallas_call(
        _attn_body,
        grid=(b, npair, s // ATT_TILE),
        in_specs=[cur, cur, cur, prev, prev],
        out_specs=cur,
        out_shape=jax.ShapeDtypeStruct((b, npair, s, LANES), BF16),
        scratch_shapes=[
            pltpu.VMEM((ATT_TILE, LANES), F32),
            pltpu.VMEM((2 * ATT_TILE, LANES), F32),
            pltpu.VMEM((2 * ATT_TILE, LANES), F32),
            pltpu.VMEM((ATT_TILE, LANES), F32),
            pltpu.VMEM((ATT_TILE, LANES), F32),
            pltpu.VMEM((ATT_TILE, LANES), F32),
        ],
        compiler_params=_cparams(("parallel", "parallel", "parallel")),
        name="dilated_attn",
    )(q, k, v, k, v)


def _log_sigmoid(z):
    return jnp.minimum(z, 0.0) - jnp.log(1.0 + jnp.exp(-jnp.abs(z)))


def _mlstm_body(qk_ref, v_ref, o_ref, gcol_ref, grow_ref, convw_ref, bcol_ref, brow_ref,
                y_ref, xc_ref, c_ref, n_ref, m_ref):
    L = ML_CHUNK
    chunk = pl.program_id(1)

    @pl.when(chunk == 0)
    def _():
        xc_ref[0:CONV_HALO, :] = jnp.zeros((CONV_HALO, 2 * D_ML), F32)
        c_ref[...] = jnp.zeros_like(c_ref)
        n_ref[...] = jnp.zeros_like(n_ref)
        m_ref[...] = jnp.zeros_like(m_ref)

    xc_ref[CONV_HALO:CONV_HALO + L, :] = qk_ref[...].astype(F32)
    w = convw_ref[...]
    conv = jnp.zeros((L, 2 * D_ML), F32)
    for j in range(CONV_W):
        off = CONV_HALO - (CONV_W - 1) + j
        conv = conv + w[j:j + 1, :] * xc_ref[off:off + L, :]
    xc_ref[0:CONV_HALO, :] = xc_ref[L:L + CONV_HALO, :]
    qkc = conv * jax.nn.sigmoid(conv)

    gc = gcol_ref[...] + bcol_ref[...]
    gr = grow_ref[...] + brow_ref[...]
    i_col, f_col = gc[:, :N_HEADS_ML], _log_sigmoid(gc[:, N_HEADS_ML:])
    i_row, f_row = gr[:N_HEADS_ML, :], _log_sigmoid(gr[N_HEADS_ML:, :])
    t_idx = lax.broadcasted_iota(jnp.int32, (L, L), 0)
    s_idx = lax.broadcasted_iota(jnp.int32, (L, L), 1)
    causal = s_idx <= t_idx

    for h in range(N_HEADS_ML):
        ln = slice(h * HD_ML, (h + 1) * HD_ML)
        q = qkc[:, ln]
        k = qkc[:, D_ML + h * HD_ML:D_ML + (h + 1) * HD_ML] * (HD_ML ** -0.5)
        v = v_ref[:, ln]
        qb, kb = q.astype(BF16), k.astype(BF16)
        b_col = jnp.sum(jnp.where(causal, f_row[h:h + 1, :], 0.0), axis=1, keepdims=True)
        b_row = jnp.sum(jnp.where(t_idx <= s_idx, f_col[:, h:h + 1], 0.0), axis=0, keepdims=True)
        a_row = i_row[h:h + 1, :] - b_row
        a_col = i_col[:, h:h + 1] - b_col
        m_st = m_ref[h][0:1, 0:1]
        a_mat = jnp.where(causal, a_row, -jnp.inf)
        m_col = jnp.maximum(jnp.max(a_mat, axis=1, keepdims=True), m_st)
        w_intra = jnp.exp(a_mat - m_col)
        w_inter = jnp.exp(m_st - m_col)
        qk = lax.dot_general(qb, kb, (((1,), (1,)), ((), ())), preferred_element_type=F32) * w_intra
        c_st = c_ref[h]
        n_st = n_ref[h][0:1, :]
        num = (jnp.dot(qk.astype(BF16), v, preferred_element_type=F32)
               + w_inter * jnp.dot(qb, c_st.astype(BF16), preferred_element_type=F32))
        den = (jnp.sum(qk, axis=1, keepdims=True)
               + w_inter * jnp.sum(q * n_st, axis=1, keepdims=True))
        h_out = num / jnp.maximum(jnp.abs(den), jnp.exp(-(b_col + m_col)))
        m_last = m_col[L - 1:L, :]
        b_last = b_col[L - 1:L, :]
        w_s = jnp.exp(a_col - m_last)
        w_c = jnp.exp(m_st - m_last)
        vs = (w_s * v.astype(F32)).astype(BF16)
        c_ref[h] = w_c * c_st + lax.dot_general(kb, vs, (((0,), (0,)), ((), ())),
                                                preferred_element_type=F32)
        n_new = w_c * n_st + jnp.sum(w_s * k, axis=0, keepdims=True)
        n_ref[h] = jnp.broadcast_to(n_new, (8, HD_ML))
        m_ref[h] = jnp.broadcast_to(b_last + m_last, (8, LANES))
        y_ref[:, ln] = (jax.nn.sigmoid(o_ref[:, ln].astype(F32)) * h_out).astype(y_ref.dtype)


def _mlstm(ml, gcol, grow, convw, bcol, brow, layer, b, s):
    L = ML_CHUNK
    nc = s // L
    lw = lambda shape: pl.BlockSpec((None,) + shape, lambda bi, i: (layer,) + (0,) * len(shape))
    return pl.pallas_call(
        _mlstm_body,
        grid=(b, nc),
        in_specs=[
            pl.BlockSpec((L, 2 * D_ML), lambda bi, i: (bi * nc + i, 0)),
            pl.BlockSpec((L, D_ML), lambda bi, i: (bi * nc + i, 2)),
            pl.BlockSpec((L, D_ML), lambda bi, i: (bi * nc + i, 3)),
            pl.BlockSpec((L, 2 * N_HEADS_ML), lambda bi, i: (bi * nc + i, 0)),
            pl.BlockSpec((None, 2 * N_HEADS_ML, L), lambda bi, i: (bi, 0, i)),
            lw((CONV_W, 2 * D_ML)), lw((1, 2 * N_HEADS_ML)), lw((2 * N_HEADS_ML, 1)),
        ],
        out_specs=pl.BlockSpec((L, D_ML), lambda bi, i: (bi * nc + i, 0)),
        out_shape=jax.ShapeDtypeStruct((b * s, D_ML), BF16),
        scratch_shapes=[
            pltpu.VMEM((L + CONV_HALO, 2 * D_ML), F32),
            pltpu.VMEM((N_HEADS_ML, HD_ML, HD_ML), F32),
            pltpu.VMEM((N_HEADS_ML, 8, HD_ML), F32),
            pltpu.VMEM((N_HEADS_ML, 8, LANES), F32),
        ],
        compiler_params=_cparams(("parallel", "arbitrary")),
        name="mlstm",
    )(ml, ml, ml, gcol, grow, convw, bcol, brow)


def _outproj_body(x_ref, ya_ref, yml_ref, wout_ref, g_ref, wr_ref, br_ref,
                  xnew_ref, hn_ref, ri_ref, rg_ref, cnt_ref):
    tm = x_ref.shape[0]

    @pl.when(pl.program_id(0) == 0)
    def _():
        cnt_ref[...] = jnp.zeros_like(cnt_ref)

    y = jnp.concatenate([ya_ref[p] for p in range(N_PAIRS)] + [yml_ref[...]], axis=-1)
    xnew = x_ref[...] + jnp.dot(y, wout_ref[...], preferred_element_type=F32)
    xnew_ref[...] = xnew
    ms = jnp.mean(xnew * xnew, axis=-1, keepdims=True)
    hn = xnew * lax.rsqrt(ms + EPS) * g_ref[...]
    hn_ref[...] = hn.astype(hn_ref.dtype)

    logits = jnp.dot(hn, wr_ref[...], preferred_element_type=F32,
                     precision=lax.Precision.HIGHEST) + br_ref[...]
    lane = lax.broadcasted_iota(jnp.int32, (tm, LANES), 1)
    lane_f = lane.astype(F32)
    big = float(LANES)

    def first_max(vals, valid):
        masked = jnp.where(valid, vals, -jnp.inf)
        top = jnp.max(masked, axis=-1, keepdims=True)
        idx = jnp.min(jnp.where(valid & (masked == top), lane_f, big), axis=-1, keepdims=True)
        return top, idx

    is_group = lane < N_GROUPS
    g_top, g_sel = first_max(logits, is_group)
    g_gate = 1.0 / jnp.sum(jnp.where(is_group, jnp.exp(logits - g_top), 0.0), axis=-1, keepdims=True)
    lo = N_GROUPS + EXPERTS_PER_GROUP * g_sel
    in_group = (lane_f >= lo) & (lane_f < lo + EXPERTS_PER_GROUP)
    t1, i1 = first_max(logits, in_group)
    t2, i2 = first_max(logits, in_group & (lane_f != i1))
    r = jnp.exp(t2 - t1)
    p1 = 1.0 / (1.0 + r)
    e1 = i1 - N_GROUPS
    e2 = i2 - N_GROUPS

    hit1 = lane_f == e1
    hit2 = lane_f == e2
    onehot = jnp.where(hit1 | hit2, 1.0, 0.0)
    ti = lax.broadcasted_iota(jnp.int32, (tm, tm), 0)
    tj = lax.broadcasted_iota(jnp.int32, (tm, tm), 1)
    strict_lower = jnp.where(tj < ti, 1.0, 0.0).astype(BF16)
    before = jnp.dot(strict_lower, onehot.astype(BF16), preferred_element_type=F32) + cnt_ref[...]
    rank1 = jnp.sum(jnp.where(hit1, before, 0.0), axis=-1, keepdims=True)
    rank2 = jnp.sum(jnp.where(hit2, before, 0.0), axis=-1, keepdims=True)
    cnt_ref[...] = cnt_ref[...] + jnp.sum(onehot, axis=0, keepdims=True)

    ri = jnp.where(lane == 0, e1, jnp.where(lane == 1, e2, jnp.where(lane == 2, rank1,
                   jnp.where(lane == 3, rank2, 0.0))))
    ri_ref[...] = ri.astype(jnp.int32)
    rg_ref[...] = jnp.where(lane == 0, g_gate * p1, jnp.where(lane == 1, g_gate * r * p1, 0.0))


def _outproj_router(x, ya, yml, wout, g, wr, br, layer, tm=512):
    b, s, d = x.shape
    ns = s // tm
    t = b * s
    xf = x.reshape(t, d)
    lw = lambda shape: pl.BlockSpec((None,) + shape, lambda i: (layer,) + (0,) * len(shape))
    row = lambda width: pl.BlockSpec((tm, width), lambda i: (i, 0))
    return pl.pallas_call(
        _outproj_body,
        grid=(t // tm,),
        in_specs=[
            row(d),
            pl.BlockSpec((None, N_PAIRS, tm, LANES), lambda i: (i // ns, 0, i % ns, 0)),
            row(D_ML),
            lw((d, d)), lw((1, d)), lw((d, LANES)), lw((1, LANES)),
        ],
        out_specs=[row(d), row(d), row(LANES), row(LANES), pl.BlockSpec((1, LANES), lambda i: (0, 0))],
        out_shape=[
            jax.ShapeDtypeStruct((t, d), F32),
            jax.ShapeDtypeStruct((t, d), BF16),
            jax.ShapeDtypeStruct((t, LANES), jnp.int32),
            jax.ShapeDtypeStruct((t, LANES), F32),
            jax.ShapeDtypeStruct((1, LANES), F32),
        ],
        compiler_params=_cparams(("arbitrary",)),
        name="outproj_router",
    )(xf, ya, yml, wout, g, wr, br)


def _expert_body(be_ref, nu_ref, xs_ref, w1_ref, w3_ref, w2_ref, yb_ref, w1b, w3b, w2b):
    i = pl.program_id(0)
    e = be_ref[i]
    e_prev = be_ref[jnp.maximum(i - 1, 0)]

    @pl.when((i == 0) | (e != e_prev))
    def _():
        w1b[...] = w1_ref[...].astype(BF16)
        w3b[...] = w3_ref[...].astype(BF16)
        w2b[...] = w2_ref[...].astype(BF16)

    @pl.when(i < nu_ref[0])
    def _():
        x = xs_ref[...]
        a = jnp.dot(x, w1b[...], preferred_element_type=F32)
        g = jnp.dot(x, w3b[...], preferred_element_type=F32)
        hdn = (a * jax.nn.sigmoid(a) * g).astype(BF16)
        yb_ref[...] = jnp.dot(hdn, w2b[...], preferred_element_type=F32).astype(yb_ref.dtype)

    @pl.when(i >= nu_ref[0])
    def _():
        yb_ref[...] = jnp.zeros_like(yb_ref)


def _experts(xs, blk_expert, n_used, w1, w3, w2, layer):
    cap, d = xs.shape
    n_blk = cap // MOE_BLOCK
    grid_spec = pltpu.PrefetchScalarGridSpec(
        num_scalar_prefetch=2,
        grid=(n_blk,),
        in_specs=[
            pl.BlockSpec((MOE_BLOCK, d), lambda i, be, nu: (i, 0)),
            pl.BlockSpec((None, None, d, D_EXPERT), lambda i, be, nu: (layer, be[i], 0, 0)),
            pl.BlockSpec((None, None, d, D_EXPERT), lambda i, be, nu: (layer, be[i], 0, 0)),
            pl.BlockSpec((None, None, D_EXPERT, d), lambda i, be, nu: (layer, be[i], 0, 0)),
        ],
        out_specs=pl.BlockSpec((MOE_BLOCK, d), lambda i, be, nu: (i, 0)),
        scratch_shapes=[
            pltpu.VMEM((d, D_EXPERT), BF16),
            pltpu.VMEM((d, D_EXPERT), BF16),
            pltpu.VMEM((D_EXPERT, d), BF16),
        ],
    )
    return pl.pallas_call(
        _expert_body,
        grid_spec=grid_spec,
        out_shape=jax.ShapeDtypeStruct((cap, d), BF16),
        compiler_params=_cparams(("arbitrary",)),
        name="experts",
    )(blk_expert, n_used, xs, w1, w3, w2)


def _combine_body(x_ref, y0_ref, y1_ref, rg_ref, o_ref):
    g = rg_ref[...]
    o_ref[...] = (x_ref[...] + g[:, 0:1] * y0_ref[...].astype(F32)
                  + g[:, 1:2] * y1_ref[...].astype(F32))


def _combine(xnew, y0, y1, rg, tm=1024):
    t, d = xnew.shape
    row = lambda width: pl.BlockSpec((tm, width), lambda i: (i, 0))
    return pl.pallas_call(
        _combine_body,
        grid=(t // tm,),
        in_specs=[row(d), row(d), row(d), row(LANES)],
        out_specs=row(d),
        out_shape=jax.ShapeDtypeStruct((t, d), F32),
        compiler_params=_cparams(("parallel",)),
        name="combine",
    )(xnew, y0, y1, rg)


def _moe_dispatch_plan(ri, cnt, t):
    n_blk = (t * TOP_K) // MOE_BLOCK + N_EXPERTS
    counts = cnt[0, :N_EXPERTS].astype(jnp.int32)
    padded = (counts + MOE_BLOCK - 1) // MOE_BLOCK * MOE_BLOCK
    pend = jnp.cumsum(padded)
    pstart = pend - padded
    dest = pstart[ri[:, 0:2]] + ri[:, 2:4]
    blk_start = jnp.arange(n_blk, dtype=jnp.int32) * MOE_BLOCK
    blk_expert = jnp.minimum(jnp.sum((pend[None, :] <= blk_start[:, None]).astype(jnp.int32), axis=1),
                             N_EXPERTS - 1)
    n_used = (pend[-1:] // MOE_BLOCK).astype(jnp.int32)
    return dest, blk_expert, n_used, n_blk


def _layer(x, layer, p):
    b, s, d = x.shape
    t = b * s
    q, k, v, ml, gcol, grow = _inproj(x, p["g_mix"], p["w_att"], p["w_ml"], p["w_g"], p["w_gt"],
                                      p["bd"], p["gq"], p["gk"], layer)
    ya = _attention(q, k, v)
    yml = _mlstm(ml, gcol, grow, p["conv_w"], p["b_col"], p["b_row"], layer, b, s)
    xnew, hn, ri, rg, cnt = _outproj_router(x, ya, yml, p["w_out"], p["g_ffn"], p["w_r"], p["b_r"], layer)
    dest, blk_expert, n_used, n_blk = _moe_dispatch_plan(ri, cnt, t)
    tok = jnp.broadcast_to(jnp.arange(t, dtype=jnp.int32)[:, None], (t, TOP_K))
    buf_tok = jnp.zeros((n_blk * MOE_BLOCK,), jnp.int32).at[dest.reshape(-1)].set(tok.reshape(-1))
    xs = hn[buf_tok]
    yb = _experts(xs, blk_expert, n_used, p["w1"], p["w3"], p["w2"], layer)
    out = _combine(xnew, yb[dest[:, 0]], yb[dest[:, 1]], rg)
    return out.reshape(b, s, d)


def _prep(g_norm_mix, w_in, b_igate, b_fgate, conv_w, g_q, g_k, w_out, g_norm_ffn,
          w_group, b_group, w_expert_router, b_expert_router, w1, w3, w2):
    w_in_b = w_in.astype(BF16)
    n_gate = 2 * N_HEADS_ML
    hd_id = jnp.arange(D_ATT, dtype=jnp.int32) // HD_ATT
    pad_r = LANES - N_GROUPS - N_EXPERTS
    p = {
        "g_mix": g_norm_mix[:, None, :],
        "w_att": w_in_b[:, :, :3 * D_ATT],
        "w_ml": w_in_b[:, :, 3 * D_ATT:3 * D_ATT + 4 * D_ML],
        "w_g": w_in_b[:, :, -n_gate:],
        "w_gt": jnp.swapaxes(w_in_b[:, :, -n_gate:], 1, 2),
        "bd": jnp.where(hd_id[:, None] == hd_id[None, :], 1.0 / HD_ATT, 0.0).astype(BF16),
        "gq": (jnp.tile(g_q, (1, N_HEADS_ATT)) * (HD_ATT ** -0.5))[:, None, :],
        "gk": jnp.tile(g_k, (1, N_HEADS_ATT))[:, None, :],
        "conv_w": conv_w,
        "b_col": jnp.concatenate([b_igate, b_fgate], axis=-1)[:, None, :],
        "b_row": jnp.concatenate([b_igate, b_fgate], axis=-1)[:, :, None],
        "w_out": w_out.astype(BF16),
        "g_ffn": g_norm_ffn[:, None, :],
        "w_r": jnp.pad(jnp.concatenate([w_group, w_expert_router], axis=-1), ((0, 0), (0, 0), (0, pad_r))),
        "b_r": jnp.pad(jnp.concatenate([b_group, b_expert_router], axis=-1), ((0, 0), (0, pad_r)))[:, None, :],
        "w1": w1, "w3": w3, "w2": w2,
    }
    return p


def kernel(x, g_norm_mix, w_in, b_igate, b_fgate, conv_w, g_q, g_k, w_out, g_norm_ffn,
           w_group, b_group, w_expert_router, b_expert_router, w1, w3, w2):
    p = _prep(g_norm_mix, w_in, b_igate, b_fgate, conv_w, g_q, g_k, w_out, g_norm_ffn,
              w_group, b_group, w_expert_router, b_expert_router, w1, w3, w2)
    for layer in range(w_in.shape[0]):
        x = _layer(x, layer, p)
    return x
```

```python
import functools

import jax
import jax.numpy as jnp
from jax import lax
from jax.experimental import pallas as pl
from jax.experimental.pallas import tpu as pltpu

F32 = jnp.float32
BF16 = jnp.bfloat16

EPS = 1e-6
D_MODEL = 1024
N_HEADS_ATT = 8
HD_ATT = 64
D_ATT = N_HEADS_ATT * HD_ATT
N_PAIRS = D_ATT // 128
DILATIONS = (1, 4, 16)
BAND = 128
ATT_TILE = BAND * 16
N_HEADS_ML = 4
HD_ML = 128
D_ML = N_HEADS_ML * HD_ML
CONV_W = 4
CONV_HALO = 16
ML_CHUNK = 256
N_GROUPS = 4
EXPERTS_PER_GROUP = 8
N_EXPERTS = N_GROUPS * EXPERTS_PER_GROUP
TOP_K = 2
D_EXPERT = 512
MOE_BLOCK = 256
LANES = 128
NEG = -1e30
VMEM_LIMIT = 56 * 1024 * 1024


def _cparams(sem):
    return pltpu.CompilerParams(dimension_semantics=sem, vmem_limit_bytes=VMEM_LIMIT)


def _inproj_body(*refs, fused_combine):
    if fused_combine:
        (x_ref, y0_ref, y1_ref, rg_ref, g_ref, watt_ref, wml_ref, wg_ref, wgt_ref, bd_ref, gq_ref, gk_ref,
         xo_ref, q_ref, k_ref, v_ref, ml_ref, gcol_ref, grow_ref) = refs
        rg = rg_ref[...]
        x = (x_ref[...] + rg[:, 0:1] * y0_ref[...].astype(F32) + rg[:, 1:2] * y1_ref[...].astype(F32))
        xo_ref[...] = x
    else:
        (x_ref, g_ref, watt_ref, wml_ref, wg_ref, wgt_ref, bd_ref, gq_ref, gk_ref,
         q_ref, k_ref, v_ref, ml_ref, gcol_ref, grow_ref) = refs
        x = x_ref[...]
    ms = jnp.mean(x * x, axis=-1, keepdims=True)
    xn = (x * lax.rsqrt(ms + EPS) * g_ref[...]).astype(BF16)
    att = jnp.dot(xn, watt_ref[...], preferred_element_type=F32)

    def headnorm(t, gain):
        msq = jnp.dot((t * t).astype(BF16), bd_ref[...], preferred_element_type=F32)
        return t * lax.rsqrt(msq + EPS) * gain

    q = headnorm(att[:, :D_ATT], gq_ref[...])
    k = headnorm(att[:, D_ATT:2 * D_ATT], gk_ref[...])
    v = att[:, 2 * D_ATT:]
    for p in range(N_PAIRS):
        sl = slice(LANES * p, LANES * (p + 1))
        q_ref[p] = q[:, sl].astype(BF16)
        k_ref[p] = k[:, sl].astype(BF16)
        v_ref[p] = v[:, sl].astype(BF16)
    ml_ref[...] = jnp.dot(xn, wml_ref[...], preferred_element_type=F32).astype(BF16)
    gcol_ref[...] = jnp.dot(xn, wg_ref[...], preferred_element_type=F32)
    grow_ref[...] = lax.dot_general(wgt_ref[...], xn, (((1,), (1,)), ((), ())),
                                    preferred_element_type=F32)


def _inproj(xf, moe, p, layer, b, s, tm=512):
    d = xf.shape[1]
    ns = s // tm
    pair_spec = pl.BlockSpec((None, N_PAIRS, tm, LANES), lambda i: (i // ns, 0, i % ns, 0))
    pair_shape = jax.ShapeDtypeStruct((b, N_PAIRS, s, LANES), BF16)
    lw = lambda shape: pl.BlockSpec((None,) + shape, lambda i: (layer,) + (0,) * len(shape))
    row = lambda width: pl.BlockSpec((tm, width), lambda i: (i, 0))
    in_specs = [
        lw((1, d)), lw((d, 3 * D_ATT)), lw((d, 4 * D_ML)), lw((d, 2 * N_HEADS_ML)),
        lw((2 * N_HEADS_ML, d)),
        pl.BlockSpec((D_ATT, D_ATT), lambda i: (0, 0)),
        lw((1, D_ATT)), lw((1, D_ATT)),
    ]
    out_specs = [
        pair_spec, pair_spec, pair_spec,
        row(4 * D_ML), row(2 * N_HEADS_ML),
        pl.BlockSpec((None, 2 * N_HEADS_ML, tm), lambda i: (i // ns, 0, i % ns)),
    ]
    out_shape = [
        pair_shape, pair_shape, pair_shape,
        jax.ShapeDtypeStruct((b * s, 4 * D_ML), BF16),
        jax.ShapeDtypeStruct((b * s, 2 * N_HEADS_ML), F32),
        jax.ShapeDtypeStruct((b, 2 * N_HEADS_ML, s), F32),
    ]
    args = [p["g_mix"], p["w_att"], p["w_ml"], p["w_g"], p["w_gt"], p["bd"], p["gq"], p["gk"]]
    if moe is None:
        in_specs = [row(d)] + in_specs
        args = [xf] + args
    else:
        in_specs = [row(d), row(d), row(d), row(LANES)] + in_specs
        args = [xf, *moe] + args
        out_specs = [row(d)] + out_specs
        out_shape = [jax.ShapeDtypeStruct((b * s, d), F32)] + out_shape
    outs = pl.pallas_call(
        functools.partial(_inproj_body, fused_combine=moe is not None),
        grid=(b * ns,),
        in_specs=in_specs,
        out_specs=out_specs,
        out_shape=out_shape,
        compiler_params=_cparams(("parallel",)),
        name="inproj",
    )(*args)
    return outs if moe is not None else [xf] + list(outs)


ATT_SAFE_LOGIT_BOUND = 40.0


def _attn_body(bound_ref, q_ref, k_ref, v_ref, kp_ref, vp_ref, o_ref, qf, kf, vf, acc_ref, m_ref, l_ref,
               *, fixed_shift):
    tile = pl.program_id(2)
    lane = lax.broadcasted_iota(jnp.int32, (1, LANES), 1)
    first_head = lane < HD_ATT
    qi = lax.broadcasted_iota(jnp.int32, (BAND, 2 * BAND), 0)
    kj = lax.broadcasted_iota(jnp.int32, (BAND, 2 * BAND), 1)
    band = (kj >= qi) & (kj <= qi + BAND)
    shift = -bound_ref[0] if fixed_shift else 0.0
    bias_prev = jnp.where(band, shift, NEG).astype(F32)
    bias_first = jnp.where(band & ((kj >= BAND) | (tile > 0)), shift, NEG).astype(F32)

    qf[...] = q_ref[...].astype(F32)
    kf[0:ATT_TILE, :] = kp_ref[...].astype(F32)
    kf[ATT_TILE:, :] = k_ref[...].astype(F32)
    vf[0:ATT_TILE, :] = vp_ref[...].astype(F32)
    vf[ATT_TILE:, :] = v_ref[...].astype(F32)
    ones = jnp.ones((2 * BAND, LANES), BF16)

    def unit(d, r, n, mode):
        q0 = n * BAND * d + r
        k0 = ATT_TILE + (n - 1) * BAND * d + r
        rows = pl.ds(q0, BAND, stride=d) if d > 1 else pl.ds(q0, BAND)
        krows = pl.ds(k0, 2 * BAND, stride=d) if d > 1 else pl.ds(k0, 2 * BAND)
        q = qf[rows, :].astype(BF16)
        kcat = kf[krows, :].astype(BF16)
        vcat = vf[krows, :].astype(BF16)
        bias = bias_first if n == 0 else bias_prev
        if fixed_shift:
            vones = jnp.concatenate([vcat, ones], axis=1)
            outs = []
            for h in range(2):
                qh = jnp.where(first_head if h == 0 else ~first_head, q, jnp.zeros_like(q))
                s = lax.dot_general(qh, kcat, (((1,), (1,)), ((), ())), preferred_element_type=F32) + bias
                outs.append(jnp.dot(jnp.exp(s).astype(BF16), vones, preferred_element_type=F32))
            acc = jnp.where(first_head, outs[0][:, :LANES], outs[1][:, :LANES])
            l_pair = jnp.where(first_head, outs[0][:, LANES:], outs[1][:, LANES:])
            if mode != "init":
                acc = acc + acc_ref[rows, :]
                l_pair = l_pair + l_ref[rows, :]
            if mode == "final":
                o_ref[rows, :] = (acc / l_pair).astype(o_ref.dtype)
            else:
                l_ref[rows, :] = l_pair
                acc_ref[rows, :] = acc
            return
        if mode != "init":
            m_old = m_ref[rows, :]
            l_old = l_ref[rows, :]
        stats = []
        for h in range(2):
            qh = jnp.where(first_head if h == 0 else ~first_head, q, jnp.zeros_like(q))
            s = lax.dot_general(qh, kcat, (((1,), (1,)), ((), ())), preferred_element_type=F32) + bias
            m_new = jnp.max(s, axis=-1, keepdims=True)
            if mode != "init":
                m_new = jnp.maximum(m_new, m_old[:, h * HD_ATT:h * HD_ATT + 1])
            p = jnp.exp(s - m_new)
            l_new = jnp.sum(p, axis=-1, keepdims=True)
            pv = jnp.dot(p.astype(BF16), vcat, preferred_element_type=F32)
            stats.append((m_new, l_new, pv))
        (m0, l0, pv0), (m1, l1, pv1) = stats
        m_pair = jnp.where(first_head, m0, m1)
        l_pair = jnp.where(first_head, l0, l1)
        acc = jnp.where(first_head, pv0, pv1)
        if mode != "init":
            alpha = jnp.exp(m_old - m_pair)
            l_pair = l_pair + alpha * l_old
            acc = acc + alpha * acc_ref[rows, :]
        if mode == "final":
            o_ref[rows, :] = (acc / l_pair).astype(o_ref.dtype)
        else:
            m_ref[rows, :] = m_pair
            l_ref[rows, :] = l_pair
            acc_ref[rows, :] = acc

    for d, mode in ((16, "init"), (4, "update"), (1, "final")):
        for r in range(d):
            for n in range(ATT_TILE // (BAND * d)):
                unit(d, r, n, mode)


def _attention(q, k, v, logit_bound):
    b, npair, s, _ = q.shape
    cur = pl.BlockSpec((None, None, ATT_TILE, LANES), lambda bi, p, i: (bi, p, i, 0))
    prev = pl.BlockSpec((None, None, ATT_TILE, LANES), lambda bi, p, i: (bi, p, jnp.maximum(i - 1, 0), 0))

    def run(fixed_shift):
        return pl.pallas_call(
            functools.partial(_attn_body, fixed_shift=fixed_shift),
            grid=(b, npair, s // ATT_TILE),
            in_specs=[pl.BlockSpec(memory_space=pltpu.SMEM), cur, cur, cur, prev, prev],
            out_specs=cur,
            out_shape=jax.ShapeDtypeStruct((b, npair, s, LANES), BF16),
            scratch_shapes=[
                pltpu.VMEM((ATT_TILE, LANES), F32),
                pltpu.VMEM((2 * ATT_TILE, LANES), F32),
                pltpu.VMEM((2 * ATT_TILE, LANES), F32),
                pltpu.VMEM((ATT_TILE, LANES), F32),
                pltpu.VMEM((ATT_TILE, LANES), F32),
                pltpu.VMEM((ATT_TILE, LANES), F32),
            ],
            compiler_params=_cparams(("parallel", "parallel", "parallel")),
            name="dilated_attn_fixed_shift" if fixed_shift else "dilated_attn",
        )(logit_bound, q, k, v, k, v)

    return lax.cond(logit_bound[0] <= ATT_SAFE_LOGIT_BOUND, lambda: run(True), lambda: run(False))


def _log_sigmoid(z):
    return jnp.minimum(z, 0.0) - jnp.log(1.0 + jnp.exp(-jnp.abs(z)))


def _mlstm_body(qk_ref, v_ref, o_ref, gcol_ref, grow_ref, convw_ref, bcol_ref, brow_ref,
                y_ref, xc_ref, c_ref, n_ref, m_ref):
    L = ML_CHUNK
    chunk = pl.program_id(1)

    @pl.when(chunk == 0)
    def _():
        xc_ref[0:CONV_HALO, :] = jnp.zeros((CONV_HALO, 2 * D_ML), F32)
        c_ref[...] = jnp.zeros_like(c_ref)
        n_ref[...] = jnp.zeros_like(n_ref)
        m_ref[...] = jnp.zeros_like(m_ref)

    xc_ref[CONV_HALO:CONV_HALO + L, :] = qk_ref[...].astype(F32)
    w = convw_ref[...]
    conv = jnp.zeros((L, 2 * D_ML), F32)
    for j in range(CONV_W):
        off = CONV_HALO - (CONV_W - 1) + j
        conv = conv + w[j:j + 1, :] * xc_ref[off:off + L, :]
    xc_ref[0:CONV_HALO, :] = xc_ref[L:L + CONV_HALO, :]
    qkc = conv * jax.nn.sigmoid(conv)

    gc = gcol_ref[...] + bcol_ref[...]
    gr = grow_ref[...] + brow_ref[...]
    i_col, f_col = gc[:, :N_HEADS_ML], _log_sigmoid(gc[:, N_HEADS_ML:])
    i_row, f_row = gr[:N_HEADS_ML, :], _log_sigmoid(gr[N_HEADS_ML:, :])
    t_idx = lax.broadcasted_iota(jnp.int32, (L, L), 0)
    s_idx = lax.broadcasted_iota(jnp.int32, (L, L), 1)
    causal = s_idx <= t_idx

    for h in range(N_HEADS_ML):
        ln = slice(h * HD_ML, (h + 1) * HD_ML)
        q = qkc[:, ln]
        k = qkc[:, D_ML + h * HD_ML:D_ML + (h + 1) * HD_ML] * (HD_ML ** -0.5)
        v = v_ref[:, ln]
        qb, kb = q.astype(BF16), k.astype(BF16)
        b_col = jnp.sum(jnp.where(causal, f_row[h:h + 1, :], 0.0), axis=1, keepdims=True)
        b_row = jnp.sum(jnp.where(t_idx <= s_idx, f_col[:, h:h + 1], 0.0), axis=0, keepdims=True)
        a_row = i_row[h:h + 1, :] - b_row
        a_col = i_col[:, h:h + 1] - b_col
        m_st = m_ref[h][0:1, 0:1]
        a_mat = jnp.where(causal, a_row, -jnp.inf)
        m_col = jnp.maximum(jnp.max(a_mat, axis=1, keepdims=True), m_st)
        w_intra = jnp.exp(a_mat - m_col)
        w_inter = jnp.exp(m_st - m_col)
        qk = lax.dot_general(qb, kb, (((1,), (1,)), ((), ())), preferred_element_type=F32) * w_intra
        c_st = c_ref[h]
        n_st = n_ref[h][0:1, :]
        num = (jnp.dot(qk.astype(BF16), v, preferred_element_type=F32)
               + w_inter * jnp.dot(qb, c_st.astype(BF16), preferred_element_type=F32))
        den = (jnp.sum(qk, axis=1, keepdims=True)
               + w_inter * jnp.sum(q * n_st, axis=1, keepdims=True))
        h_out = num / jnp.maximum(jnp.abs(den), jnp.exp(-(b_col + m_col)))
        m_last = m_col[L - 1:L, :]
        b_last = b_col[L - 1:L, :]
        w_s = jnp.exp(a_col - m_last)
        w_c = jnp.exp(m_st - m_last)
        vs = (w_s * v.astype(F32)).astype(BF16)
        c_ref[h] = w_c * c_st + lax.dot_general(kb, vs, (((0,), (0,)), ((), ())),
                                                preferred_element_type=F32)
        n_new = w_c * n_st + jnp.sum(w_s * k, axis=0, keepdims=True)
        n_ref[h] = jnp.broadcast_to(n_new, (8, HD_ML))
        m_ref[h] = jnp.broadcast_to(b_last + m_last, (8, LANES))
        y_ref[:, ln] = (jax.nn.sigmoid(o_ref[:, ln].astype(F32)) * h_out).astype(y_ref.dtype)


def _mlstm(ml, gcol, grow, convw, bcol, brow, layer, b, s):
    L = ML_CHUNK
    nc = s // L
    lw = lambda shape: pl.BlockSpec((None,) + shape, lambda bi, i: (layer,) + (0,) * len(shape))
    return pl.pallas_call(
        _mlstm_body,
        grid=(b, nc),
        in_specs=[
            pl.BlockSpec((L, 2 * D_ML), lambda bi, i: (bi * nc + i, 0)),
            pl.BlockSpec((L, D_ML), lambda bi, i: (bi * nc + i, 2)),
            pl.BlockSpec((L, D_ML), lambda bi, i: (bi * nc + i, 3)),
            pl.BlockSpec((L, 2 * N_HEADS_ML), lambda bi, i: (bi * nc + i, 0)),
            pl.BlockSpec((None, 2 * N_HEADS_ML, L), lambda bi, i: (bi, 0, i)),
            lw((CONV_W, 2 * D_ML)), lw((1, 2 * N_HEADS_ML)), lw((2 * N_HEADS_ML, 1)),
        ],
        out_specs=pl.BlockSpec((L, D_ML), lambda bi, i: (bi * nc + i, 0)),
        out_shape=jax.ShapeDtypeStruct((b * s, D_ML), BF16),
        scratch_shapes=[
            pltpu.VMEM((L + CONV_HALO, 2 * D_ML), F32),
            pltpu.VMEM((N_HEADS_ML, HD_ML, HD_ML), F32),
            pltpu.VMEM((N_HEADS_ML, 8, HD_ML), F32),
            pltpu.VMEM((N_HEADS_ML, 8, LANES), F32),
        ],
        compiler_params=_cparams(("parallel", "arbitrary")),
        name="mlstm",
    )(ml, ml, ml, gcol, grow, convw, bcol, brow)


def _outproj_body(x_ref, ya_ref, yml_ref, wout_ref, g_ref, wr_ref, br_ref,
                  xnew_ref, hn_ref, ri_ref, rg_ref, cnt_ref):
    tm = x_ref.shape[0]

    @pl.when(pl.program_id(0) == 0)
    def _():
        cnt_ref[...] = jnp.zeros_like(cnt_ref)

    y = jnp.concatenate([ya_ref[p] for p in range(N_PAIRS)] + [yml_ref[...]], axis=-1)
    xnew = x_ref[...] + jnp.dot(y, wout_ref[...], preferred_element_type=F32)
    xnew_ref[...] = xnew
    ms = jnp.mean(xnew * xnew, axis=-1, keepdims=True)
    hn = xnew * lax.rsqrt(ms + EPS) * g_ref[...]
    hn_ref[...] = hn

    hn_hi = hn.astype(BF16)
    hn_lo = (hn - hn_hi.astype(F32)).astype(BF16)
    w_hi, w_lo = wr_ref[0], wr_ref[1]
    logits = (jnp.dot(hn_hi, w_hi, preferred_element_type=F32)
              + (jnp.dot(hn_lo, w_hi, preferred_element_type=F32)
                 + jnp.dot(hn_hi, w_lo, preferred_element_type=F32))) + br_ref[...]
    lane = lax.broadcasted_iota(jnp.int32, (tm, LANES), 1)
    lane_f = lane.astype(F32)
    big = float(LANES)

    def first_max(vals, valid):
        masked = jnp.where(valid, vals, -jnp.inf)
        top = jnp.max(masked, axis=-1, keepdims=True)
        idx = jnp.min(jnp.where(valid & (masked == top), lane_f, big), axis=-1, keepdims=True)
        return top, idx

    is_group = lane < N_GROUPS
    g_top, g_sel = first_max(logits, is_group)
    g_gate = 1.0 / jnp.sum(jnp.where(is_group, jnp.exp(logits - g_top), 0.0), axis=-1, keepdims=True)
    lo = N_GROUPS + EXPERTS_PER_GROUP * g_sel
    in_group = (lane_f >= lo) & (lane_f < lo + EXPERTS_PER_GROUP)
    t1, i1 = first_max(logits, in_group)
    t2, i2 = first_max(logits, in_group & (lane_f != i1))
    r = jnp.exp(t2 - t1)
    p1 = 1.0 / (1.0 + r)
    e1 = i1 - N_GROUPS
    e2 = i2 - N_GROUPS

    hit1 = lane_f == e1
    hit2 = lane_f == e2
    onehot = jnp.where(hit1 | hit2, 1.0, 0.0)
    ti = lax.broadcasted_iota(jnp.int32, (tm, tm), 0)
    tj = lax.broadcasted_iota(jnp.int32, (tm, tm), 1)
    strict_lower = jnp.where(tj < ti, 1.0, 0.0).astype(BF16)
    before = jnp.dot(strict_lower, onehot.astype(BF16), preferred_element_type=F32) + cnt_ref[...]
    rank1 = jnp.sum(jnp.where(hit1, before, 0.0), axis=-1, keepdims=True)
    rank2 = jnp.sum(jnp.where(hit2, before, 0.0), axis=-1, keepdims=True)
    cnt_ref[...] = cnt_ref[...] + jnp.sum(onehot, axis=0, keepdims=True)

    ri = jnp.where(lane == 0, e1, jnp.where(lane == 1, e2, jnp.where(lane == 2, rank1,
                   jnp.where(lane == 3, rank2, 0.0))))
    ri_ref[...] = ri.astype(jnp.int32)
    rg_ref[...] = jnp.where(lane == 0, g_gate * p1, jnp.where(lane == 1, g_gate * r * p1, 0.0))


def _outproj_router(xf, ya, yml, wout, g, wr, br, layer, b, s, tm=512):
    t, d = xf.shape
    ns = s // tm
    lw = lambda shape: pl.BlockSpec((None,) + shape, lambda i: (layer,) + (0,) * len(shape))
    row = lambda width: pl.BlockSpec((tm, width), lambda i: (i, 0))
    return pl.pallas_call(
        _outproj_body,
        grid=(t // tm,),
        in_specs=[
            row(d),
            pl.BlockSpec((None, N_PAIRS, tm, LANES), lambda i: (i // ns, 0, i % ns, 0)),
            row(D_ML),
            lw((d, d)), lw((1, d)), lw((2, d, LANES)), lw((1, LANES)),
        ],
        out_specs=[row(d), row(d), row(LANES), row(LANES), pl.BlockSpec((1, LANES), lambda i: (0, 0))],
        out_shape=[
            jax.ShapeDtypeStruct((t, d), F32),
            jax.ShapeDtypeStruct((t, d), F32),
            jax.ShapeDtypeStruct((t, LANES), jnp.int32),
            jax.ShapeDtypeStruct((t, LANES), F32),
            jax.ShapeDtypeStruct((1, LANES), F32),
        ],
        compiler_params=_cparams(("arbitrary",)),
        name="outproj_router",
    )(xf, ya, yml, wout, g, wr, br)


DISPATCH_TILE = 512


def _dispatch_body(dest_ref, zero_ref, hn_hbm, zeros_hbm, xs_hbm, sems, zsem):
    i = pl.program_id(0)
    n = pl.num_programs(0)
    rows_per_step = TOP_K * DISPATCH_TILE

    @pl.when(i == 0)
    def _():
        def zero_copy(blk):
            row0 = pl.multiple_of(jnp.maximum(zero_ref[blk], 0), MOE_BLOCK)
            return pltpu.make_async_copy(zeros_hbm, xs_hbm.at[pl.ds(row0, MOE_BLOCK), :], zsem)

        def zstart(blk, c):
            @pl.when(zero_ref[blk] >= 0)
            def _():
                zero_copy(blk).start()
            return c

        def zwait(blk, c):
            @pl.when(zero_ref[blk] >= 0)
            def _():
                zero_copy(blk).wait()
            return c
        lax.fori_loop(0, zero_ref.shape[0], zstart, 0)
        lax.fori_loop(0, zero_ref.shape[0], zwait, 0)

    def wait_step(slot):
        pltpu.make_async_copy(xs_hbm.at[pl.ds(0, rows_per_step), :],
                              xs_hbm.at[pl.ds(0, rows_per_step), :], sems.at[slot]).wait()

    slot = i % 2

    def issue(j, c):
        t = i * DISPATCH_TILE + j
        src = hn_hbm.at[pl.ds(t, 1), :]
        for kk in range(TOP_K):
            pltpu.make_async_copy(src, xs_hbm.at[pl.ds(dest_ref[TOP_K * t + kk], 1), :],
                                  sems.at[slot]).start()
        return c
    lax.fori_loop(0, DISPATCH_TILE, issue, 0, unroll=8)

    @pl.when(i > 0)
    def _():
        wait_step(1 - slot)

    @pl.when(i == n - 1)
    def _():
        wait_step(slot)


def _dispatch(hn, dest_flat, zero_rows, cap):
    t, d = hn.shape
    grid_spec = pltpu.PrefetchScalarGridSpec(
        num_scalar_prefetch=2,
        grid=(t // DISPATCH_TILE,),
        in_specs=[pl.BlockSpec(memory_space=pl.ANY), pl.BlockSpec(memory_space=pl.ANY)],
        out_specs=pl.BlockSpec(memory_space=pl.ANY),
        scratch_shapes=[pltpu.SemaphoreType.DMA((2,)), pltpu.SemaphoreType.DMA(())],
    )
    return pl.pallas_call(
        _dispatch_body,
        grid_spec=grid_spec,
        out_shape=jax.ShapeDtypeStruct((cap, d), F32),
        compiler_params=_cparams(("arbitrary",)),
        name="dispatch",
    )(dest_flat, zero_rows, hn, jnp.zeros((MOE_BLOCK, d), F32))


def _expert_body(be_ref, bv_ref, xs_ref, w1_ref, w3_ref, w2_ref, yb_ref, w1b, w3b, w2b):
    i = pl.program_id(0)
    e = be_ref[i]
    e_prev = be_ref[jnp.maximum(i - 1, 0)]
    valid = bv_ref[i]

    @pl.when((i == 0) | (e != e_prev))
    def _():
        w1b[...] = w1_ref[...].astype(BF16)
        w3b[...] = w3_ref[...].astype(BF16)
        w2b[...] = w2_ref[...].astype(BF16)

    @pl.when(valid > 0)
    def _():
        x = xs_ref[...].astype(BF16)
        a = jnp.dot(x, w1b[...], preferred_element_type=F32)
        g = jnp.dot(x, w3b[...], preferred_element_type=F32)
        hdn = (a * jax.nn.sigmoid(a) * g).astype(BF16)
        yb_ref[...] = jnp.dot(hdn, w2b[...], preferred_element_type=F32).astype(yb_ref.dtype)

    @pl.when(valid == 0)
    def _():
        yb_ref[...] = jnp.zeros_like(yb_ref)


def _experts(xs, blk_expert, blk_valid, w1, w3, w2, layer):
    cap, d = xs.shape
    n_blk = cap // MOE_BLOCK
    grid_spec = pltpu.PrefetchScalarGridSpec(
        num_scalar_prefetch=2,
        grid=(n_blk,),
        in_specs=[
            pl.BlockSpec((MOE_BLOCK, d), lambda i, be, nu: (i, 0)),
            pl.BlockSpec((None, None, d, D_EXPERT), lambda i, be, nu: (layer, be[i], 0, 0)),
            pl.BlockSpec((None, None, d, D_EXPERT), lambda i, be, nu: (layer, be[i], 0, 0)),
            pl.BlockSpec((None, None, D_EXPERT, d), lambda i, be, nu: (layer, be[i], 0, 0)),
        ],
        out_specs=pl.BlockSpec((MOE_BLOCK, d), lambda i, be, nu: (i, 0)),
        scratch_shapes=[
            pltpu.VMEM((d, D_EXPERT), BF16),
            pltpu.VMEM((d, D_EXPERT), BF16),
            pltpu.VMEM((D_EXPERT, d), BF16),
        ],
    )
    return pl.pallas_call(
        _expert_body,
        grid_spec=grid_spec,
        out_shape=jax.ShapeDtypeStruct((cap, d), BF16),
        compiler_params=_cparams(("arbitrary",)),
        name="experts",
    )(blk_expert, blk_valid, xs, w1, w3, w2)


def _combine_body(x_ref, y0_ref, y1_ref, rg_ref, o_ref):
    g = rg_ref[...]
    o_ref[...] = (x_ref[...] + g[:, 0:1] * y0_ref[...].astype(F32)
                  + g[:, 1:2] * y1_ref[...].astype(F32))


def _combine(xnew, y0, y1, rg, tm=1024):
    t, d = xnew.shape
    row = lambda width: pl.BlockSpec((tm, width), lambda i: (i, 0))
    return pl.pallas_call(
        _combine_body,
        grid=(t // tm,),
        in_specs=[row(d), row(d), row(d), row(LANES)],
        out_specs=row(d),
        out_shape=jax.ShapeDtypeStruct((t, d), F32),
        compiler_params=_cparams(("parallel",)),
        name="combine",
    )(xnew, y0, y1, rg)


def _moe_dispatch_plan(ri, cnt, t):
    n_blk = (t * TOP_K) // MOE_BLOCK + N_EXPERTS
    counts = cnt[0, :N_EXPERTS].astype(jnp.int32)
    padded = (counts + MOE_BLOCK - 1) // MOE_BLOCK * MOE_BLOCK
    pend = jnp.cumsum(padded)
    pstart = pend - padded
    dest = pstart[ri[:, 0:2]] + ri[:, 2:4]
    blk_start = jnp.arange(n_blk, dtype=jnp.int32) * MOE_BLOCK
    blk_expert = jnp.minimum(jnp.sum((pend[None, :] <= blk_start[:, None]).astype(jnp.int32), axis=1),
                             N_EXPERTS - 1)
    blk_valid = jnp.clip((pstart + counts)[blk_expert] - blk_start, 0, MOE_BLOCK)
    zero_rows = jnp.where(blk_valid < MOE_BLOCK, blk_start, -1)
    return dest, blk_expert, blk_valid, zero_rows, n_blk


def _layer(xf, moe, layer, p, b, s):
    t = b * s
    x, q, k, v, ml, gcol, grow = _inproj(xf, moe, p, layer, b, s)
    ya = _attention(q, k, v, p["logit_bound"][layer])
    yml = _mlstm(ml, gcol, grow, p["conv_w"], p["b_col"], p["b_row"], layer, b, s)
    xnew, hn, ri, rg, cnt = _outproj_router(x, ya, yml, p["w_out"], p["g_ffn"], p["w_r"], p["b_r"], layer, b, s)
    dest, blk_expert, blk_valid, zero_rows, n_blk = _moe_dispatch_plan(ri, cnt, t)
    xs = _dispatch(hn, dest.reshape(-1), zero_rows, n_blk * MOE_BLOCK)
    yb = _experts(xs, blk_expert, blk_valid, p["w1"], p["w3"], p["w2"], layer)
    return xnew, (yb[dest[:, 0]], yb[dest[:, 1]], rg)


def _prep(g_norm_mix, w_in, b_igate, b_fgate, conv_w, g_q, g_k, w_out, g_norm_ffn,
          w_group, b_group, w_expert_router, b_expert_router, w1, w3, w2):
    w_in_b = w_in.astype(BF16)
    n_gate = 2 * N_HEADS_ML
    hd_id = jnp.arange(D_ATT, dtype=jnp.int32) // HD_ATT
    pad_r = LANES - N_GROUPS - N_EXPERTS
    w_r = jnp.pad(jnp.concatenate([w_group, w_expert_router], axis=-1), ((0, 0), (0, 0), (0, pad_r)))
    w_r_hi = w_r.astype(BF16)
    w_r_lo = (w_r - w_r_hi.astype(F32)).astype(BF16)
    p = {
        "g_mix": g_norm_mix[:, None, :],
        "w_att": w_in_b[:, :, :3 * D_ATT],
        "w_ml": w_in_b[:, :, 3 * D_ATT:3 * D_ATT + 4 * D_ML],
        "w_g": w_in_b[:, :, -n_gate:],
        "w_gt": jnp.swapaxes(w_in_b[:, :, -n_gate:], 1, 2),
        "bd": jnp.where(hd_id[:, None] == hd_id[None, :], 1.0 / HD_ATT, 0.0).astype(BF16),
        "gq": (jnp.tile(g_q, (1, N_HEADS_ATT)) * (HD_ATT ** -0.5))[:, None, :],
        "gk": jnp.tile(g_k, (1, N_HEADS_ATT))[:, None, :],
        "logit_bound": (1.02 * HD_ATT ** 0.5 * jnp.max(jnp.abs(g_q), axis=-1)
                        * jnp.max(jnp.abs(g_k), axis=-1))[:, None],
        "conv_w": conv_w,
        "b_col": jnp.concatenate([b_igate, b_fgate], axis=-1)[:, None, :],
        "b_row": jnp.concatenate([b_igate, b_fgate], axis=-1)[:, :, None],
        "w_out": w_out.astype(BF16),
        "g_ffn": g_norm_ffn[:, None, :],
        "w_r": jnp.stack([w_r_hi, w_r_lo], axis=1),
        "b_r": jnp.pad(jnp.concatenate([b_group, b_expert_router], axis=-1), ((0, 0), (0, pad_r)))[:, None, :],
        "w1": w1, "w3": w3, "w2": w2,
    }
    return p


def kernel(x, g_norm_mix, w_in, b_igate, b_fgate, conv_w, g_q, g_k, w_out, g_norm_ffn,
           w_group, b_group, w_expert_router, b_expert_router, w1, w3, w2):
    p = _prep(g_norm_mix, w_in, b_igate, b_fgate, conv_w, g_q, g_k, w_out, g_norm_ffn,
              w_group, b_group, w_expert_router, b_expert_router, w1, w3, w2)
    b, s, d = x.shape
    xf, moe = x.reshape(b * s, d), None
    for layer in range(w_in.shape[0]):
        xf, moe = _layer(xf, moe, layer, p, b, s)
    return _combine(xf, *moe).reshape(b, s, d)
```

```python
import functools

import jax
import jax.numpy as jnp
from jax import lax
from jax.experimental import pallas as pl
from jax.experimental.pallas import tpu as pltpu

F32 = jnp.float32
BF16 = jnp.bfloat16

EPS = 1e-6
D_MODEL = 1024
N_HEADS_ATT = 8
HD_ATT = 64
D_ATT = N_HEADS_ATT * HD_ATT
N_PAIRS = D_ATT // 128
DILATIONS = (1, 4, 16)
BAND = 128
ATT_TILE = BAND * 16
N_HEADS_ML = 4
HD_ML = 128
D_ML = N_HEADS_ML * HD_ML
CONV_W = 4
CONV_HALO = 16
ML_CHUNK = 256
N_GROUPS = 4
EXPERTS_PER_GROUP = 8
N_EXPERTS = N_GROUPS * EXPERTS_PER_GROUP
TOP_K = 2
D_EXPERT = 512
MOE_BLOCK = 256
LANES = 128
NEG = -1e30
VMEM_LIMIT = 56 * 1024 * 1024


def _cparams(sem):
    return pltpu.CompilerParams(dimension_semantics=sem, vmem_limit_bytes=VMEM_LIMIT)


def _inproj_body(*refs, fused_combine):
    if fused_combine:
        (x_ref, y0_ref, y1_ref, rg_ref, g_ref, watt_ref, wml_ref, wg_ref, wgt_ref, bd_ref, gq_ref, gk_ref,
         xo_ref, q_ref, k_ref, v_ref, ml_ref, gcol_ref, grow_ref) = refs
        rg = rg_ref[...]
        x = (x_ref[...] + rg[:, 0:1] * y0_ref[...].astype(F32) + rg[:, 1:2] * y1_ref[...].astype(F32))
        xo_ref[...] = x
    else:
        (x_ref, g_ref, watt_ref, wml_ref, wg_ref, wgt_ref, bd_ref, gq_ref, gk_ref,
         q_ref, k_ref, v_ref, ml_ref, gcol_ref, grow_ref) = refs
        x = x_ref[...]
    ms = jnp.mean(x * x, axis=-1, keepdims=True)
    xn = (x * lax.rsqrt(ms + EPS) * g_ref[...]).astype(BF16)
    att = jnp.dot(xn, watt_ref[...], preferred_element_type=F32)

    def headnorm(t, gain):
        msq = jnp.dot((t * t).astype(BF16), bd_ref[...], preferred_element_type=F32)
        return t * lax.rsqrt(msq + EPS) * gain

    q = headnorm(att[:, :D_ATT], gq_ref[...])
    k = headnorm(att[:, D_ATT:2 * D_ATT], gk_ref[...])
    v = att[:, 2 * D_ATT:]
    for p in range(N_PAIRS):
        sl = slice(LANES * p, LANES * (p + 1))
        q_ref[p] = q[:, sl].astype(BF16)
        k_ref[p] = k[:, sl].astype(BF16)
        v_ref[p] = v[:, sl].astype(BF16)
    ml_ref[...] = jnp.dot(xn, wml_ref[...], preferred_element_type=F32).astype(BF16)
    gcol_ref[...] = jnp.dot(xn, wg_ref[...], preferred_element_type=F32)
    grow_ref[...] = lax.dot_general(wgt_ref[...], xn, (((1,), (1,)), ((), ())),
                                    preferred_element_type=F32)


def _inproj(xf, moe, p, layer, b, s, tm=512):
    d = xf.shape[1]
    ns = s // tm
    pair_spec = pl.BlockSpec((None, N_PAIRS, tm, LANES), lambda i: (i // ns, 0, i % ns, 0))
    pair_shape = jax.ShapeDtypeStruct((b, N_PAIRS, s, LANES), BF16)
    lw = lambda shape: pl.BlockSpec((None,) + shape, lambda i: (layer,) + (0,) * len(shape))
    row = lambda width: pl.BlockSpec((tm, width), lambda i: (i, 0))
    in_specs = [
        lw((1, d)), lw((d, 3 * D_ATT)), lw((d, 4 * D_ML)), lw((d, 2 * N_HEADS_ML)),
        lw((2 * N_HEADS_ML, d)),
        pl.BlockSpec((D_ATT, D_ATT), lambda i: (0, 0)),
        lw((1, D_ATT)), lw((1, D_ATT)),
    ]
    out_specs = [
        pair_spec, pair_spec, pair_spec,
        row(4 * D_ML), row(2 * N_HEADS_ML),
        pl.BlockSpec((None, 2 * N_HEADS_ML, tm), lambda i: (i // ns, 0, i % ns)),
    ]
    out_shape = [
        pair_shape, pair_shape, pair_shape,
        jax.ShapeDtypeStruct((b * s, 4 * D_ML), BF16),
        jax.ShapeDtypeStruct((b * s, 2 * N_HEADS_ML), F32),
        jax.ShapeDtypeStruct((b, 2 * N_HEADS_ML, s), F32),
    ]
    args = [p["g_mix"], p["w_att"], p["w_ml"], p["w_g"], p["w_gt"], p["bd"], p["gq"], p["gk"]]
    if moe is None:
        in_specs = [row(d)] + in_specs
        args = [xf] + args
    else:
        in_specs = [row(d), row(d), row(d), row(LANES)] + in_specs
        args = [xf, *moe] + args
        out_specs = [row(d)] + out_specs
        out_shape = [jax.ShapeDtypeStruct((b * s, d), F32)] + out_shape
    outs = pl.pallas_call(
        functools.partial(_inproj_body, fused_combine=moe is not None),
        grid=(b * ns,),
        in_specs=in_specs,
        out_specs=out_specs,
        out_shape=out_shape,
        compiler_params=_cparams(("parallel",)),
        name="inproj",
    )(*args)
    return outs if moe is not None else [xf] + list(outs)


ATT_SAFE_LOGIT_BOUND = 40.0


def _attn_body(bound_ref, q_ref, k_ref, v_ref, kp_ref, vp_ref, o_ref, qf, kf, vf, acc_ref, m_ref, l_ref,
               *, fixed_shift):
    tile = pl.program_id(2)
    lane = lax.broadcasted_iota(jnp.int32, (1, LANES), 1)
    first_head = lane < HD_ATT
    qi = lax.broadcasted_iota(jnp.int32, (BAND, 2 * BAND), 0)
    kj = lax.broadcasted_iota(jnp.int32, (BAND, 2 * BAND), 1)
    band = (kj >= qi) & (kj <= qi + BAND)
    shift = -bound_ref[0] if fixed_shift else 0.0
    bias_prev = jnp.where(band, shift, NEG).astype(F32)
    bias_first = jnp.where(band & ((kj >= BAND) | (tile > 0)), shift, NEG).astype(F32)

    qf[...] = q_ref[...].astype(F32)
    kf[0:ATT_TILE, :] = kp_ref[...].astype(F32)
    kf[ATT_TILE:, :] = k_ref[...].astype(F32)
    vf[0:ATT_TILE, :] = vp_ref[...].astype(F32)
    vf[ATT_TILE:, :] = v_ref[...].astype(F32)
    ones = jnp.ones((2 * BAND, LANES), BF16)

    def unit(d, r, n, mode):
        q0 = n * BAND * d + r
        k0 = ATT_TILE + (n - 1) * BAND * d + r
        rows = pl.ds(q0, BAND, stride=d) if d > 1 else pl.ds(q0, BAND)
        krows = pl.ds(k0, 2 * BAND, stride=d) if d > 1 else pl.ds(k0, 2 * BAND)
        q = qf[rows, :].astype(BF16)
        kcat = kf[krows, :].astype(BF16)
        vcat = vf[krows, :].astype(BF16)
        bias = bias_first if n == 0 else bias_prev
        if fixed_shift:
            vones = jnp.concatenate([vcat, ones], axis=1)
            outs = []
            for h in range(2):
                qh = jnp.where(first_head if h == 0 else ~first_head, q, jnp.zeros_like(q))
                s = lax.dot_general(qh, kcat, (((1,), (1,)), ((), ())), preferred_element_type=F32) + bias
                outs.append(jnp.dot(jnp.exp(s).astype(BF16), vones, preferred_element_type=F32))
            acc = jnp.where(first_head, outs[0][:, :LANES], outs[1][:, :LANES])
            l_pair = jnp.where(first_head, outs[0][:, LANES:], outs[1][:, LANES:])
            if mode != "init":
                acc = acc + acc_ref[rows, :]
                l_pair = l_pair + l_ref[rows, :]
            if mode == "final":
                o_ref[rows, :] = (acc / l_pair).astype(o_ref.dtype)
            else:
                l_ref[rows, :] = l_pair
                acc_ref[rows, :] = acc
            return
        if mode != "init":
            m_old = m_ref[rows, :]
            l_old = l_ref[rows, :]
        stats = []
        for h in range(2):
            qh = jnp.where(first_head if h == 0 else ~first_head, q, jnp.zeros_like(q))
            s = lax.dot_general(qh, kcat, (((1,), (1,)), ((), ())), preferred_element_type=F32) + bias
            m_new = jnp.max(s, axis=-1, keepdims=True)
            if mode != "init":
                m_new = jnp.maximum(m_new, m_old[:, h * HD_ATT:h * HD_ATT + 1])
            p = jnp.exp(s - m_new)
            l_new = jnp.sum(p, axis=-1, keepdims=True)
            pv = jnp.dot(p.astype(BF16), vcat, preferred_element_type=F32)
            stats.append((m_new, l_new, pv))
        (m0, l0, pv0), (m1, l1, pv1) = stats
        m_pair = jnp.where(first_head, m0, m1)
        l_pair = jnp.where(first_head, l0, l1)
        acc = jnp.where(first_head, pv0, pv1)
        if mode != "init":
            alpha = jnp.exp(m_old - m_pair)
            l_pair = l_pair + alpha * l_old
            acc = acc + alpha * acc_ref[rows, :]
        if mode == "final":
            o_ref[rows, :] = (acc / l_pair).astype(o_ref.dtype)
        else:
            m_ref[rows, :] = m_pair
            l_ref[rows, :] = l_pair
            acc_ref[rows, :] = acc

    for d, mode in ((16, "init"), (4, "update"), (1, "final")):
        for r in range(d):
            for n in range(ATT_TILE // (BAND * d)):
                unit(d, r, n, mode)


def _attention(q, k, v, logit_bound):
    b, npair, s, _ = q.shape
    cur = pl.BlockSpec((None, None, ATT_TILE, LANES), lambda bi, p, i: (bi, p, i, 0))
    prev = pl.BlockSpec((None, None, ATT_TILE, LANES), lambda bi, p, i: (bi, p, jnp.maximum(i - 1, 0), 0))

    def run(fixed_shift):
        return pl.pallas_call(
            functools.partial(_attn_body, fixed_shift=fixed_shift),
            grid=(b, npair, s // ATT_TILE),
            in_specs=[pl.BlockSpec(memory_space=pltpu.SMEM), cur, cur, cur, prev, prev],
            out_specs=cur,
            out_shape=jax.ShapeDtypeStruct((b, npair, s, LANES), BF16),
            scratch_shapes=[
                pltpu.VMEM((ATT_TILE, LANES), F32),
                pltpu.VMEM((2 * ATT_TILE, LANES), F32),
                pltpu.VMEM((2 * ATT_TILE, LANES), F32),
                pltpu.VMEM((ATT_TILE, LANES), F32),
                pltpu.VMEM((ATT_TILE, LANES), F32),
                pltpu.VMEM((ATT_TILE, LANES), F32),
            ],
            compiler_params=_cparams(("parallel", "parallel", "parallel")),
            name="dilated_attn_fixed_shift" if fixed_shift else "dilated_attn",
        )(logit_bound, q, k, v, k, v)

    return lax.cond(logit_bound[0] <= ATT_SAFE_LOGIT_BOUND, lambda: run(True), lambda: run(False))


def _log_sigmoid(z):
    return jnp.minimum(z, 0.0) - jnp.log(1.0 + jnp.exp(-jnp.abs(z)))


def _mlstm_body(qk_ref, v_ref, o_ref, gcol_ref, grow_ref, convw_ref, bcol_ref, brow_ref,
                y_ref, xc_ref, c_ref, n_ref, m_ref):
    L = ML_CHUNK
    chunk = pl.program_id(1)

    @pl.when(chunk == 0)
    def _():
        xc_ref[0:CONV_HALO, :] = jnp.zeros((CONV_HALO, 2 * D_ML), F32)
        c_ref[...] = jnp.zeros_like(c_ref)
        n_ref[...] = jnp.zeros_like(n_ref)
        m_ref[...] = jnp.zeros_like(m_ref)

    xc_ref[CONV_HALO:CONV_HALO + L, :] = qk_ref[...].astype(F32)
    w = convw_ref[...]
    conv = jnp.zeros((L, 2 * D_ML), F32)
    for j in range(CONV_W):
        off = CONV_HALO - (CONV_W - 1) + j
        conv = conv + w[j:j + 1, :] * xc_ref[off:off + L, :]
    xc_ref[0:CONV_HALO, :] = xc_ref[L:L + CONV_HALO, :]
    qkc = conv * jax.nn.sigmoid(conv)

    gc = gcol_ref[...] + bcol_ref[...]
    gr = grow_ref[...] + brow_ref[...]
    i_col, f_col = gc[:, :N_HEADS_ML], _log_sigmoid(gc[:, N_HEADS_ML:])
    i_row, f_row = gr[:N_HEADS_ML, :], _log_sigmoid(gr[N_HEADS_ML:, :])
    t_idx = lax.broadcasted_iota(jnp.int32, (L, L), 0)
    s_idx = lax.broadcasted_iota(jnp.int32, (L, L), 1)
    causal = s_idx <= t_idx

    for h in range(N_HEADS_ML):
        ln = slice(h * HD_ML, (h + 1) * HD_ML)
        q = qkc[:, ln]
        k = qkc[:, D_ML + h * HD_ML:D_ML + (h + 1) * HD_ML] * (HD_ML ** -0.5)
        v = v_ref[:, ln]
        qb, kb = q.astype(BF16), k.astype(BF16)
        b_col = jnp.sum(jnp.where(causal, f_row[h:h + 1, :], 0.0), axis=1, keepdims=True)
        b_row = jnp.sum(jnp.where(t_idx <= s_idx, f_col[:, h:h + 1], 0.0), axis=0, keepdims=True)
        a_row = i_row[h:h + 1, :] - b_row
        a_col = i_col[:, h:h + 1] - b_col
        m_st = m_ref[h][0:1, 0:1]
        a_mat = jnp.where(causal, a_row, -jnp.inf)
        m_col = jnp.maximum(jnp.max(a_mat, axis=1, keepdims=True), m_st)
        w_intra = jnp.exp(a_mat - m_col)
        w_inter = jnp.exp(m_st - m_col)
        qk = lax.dot_general(qb, kb, (((1,), (1,)), ((), ())), preferred_element_type=F32) * w_intra
        c_st = c_ref[h]
        n_st = n_ref[h][0:1, :]
        num = (jnp.dot(qk.astype(BF16), v, preferred_element_type=F32)
               + w_inter * jnp.dot(qb, c_st.astype(BF16), preferred_element_type=F32))
        den = (jnp.sum(qk, axis=1, keepdims=True)
               + w_inter * jnp.sum(q * n_st, axis=1, keepdims=True))
        h_out = num / jnp.maximum(jnp.abs(den), jnp.exp(-(b_col + m_col)))
        m_last = m_col[L - 1:L, :]
        b_last = b_col[L - 1:L, :]
        w_s = jnp.exp(a_col - m_last)
        w_c = jnp.exp(m_st - m_last)
        vs = (w_s * v.astype(F32)).astype(BF16)
        c_ref[h] = w_c * c_st + lax.dot_general(kb, vs, (((0,), (0,)), ((), ())),
                                                preferred_element_type=F32)
        n_new = w_c * n_st + jnp.sum(w_s * k, axis=0, keepdims=True)
        n_ref[h] = jnp.broadcast_to(n_new, (8, HD_ML))
        m_ref[h] = jnp.broadcast_to(b_last + m_last, (8, LANES))
        y_ref[:, ln] = (jax.nn.sigmoid(o_ref[:, ln].astype(F32)) * h_out).astype(y_ref.dtype)


def _mlstm(ml, gcol, grow, convw, bcol, brow, layer, b, s):
    L = ML_CHUNK
    nc = s // L
    lw = lambda shape: pl.BlockSpec((None,) + shape, lambda bi, i: (layer,) + (0,) * len(shape))
    return pl.pallas_call(
        _mlstm_body,
        grid=(b, nc),
        in_specs=[
            pl.BlockSpec((L, 2 * D_ML), lambda bi, i: (bi * nc + i, 0)),
            pl.BlockSpec((L, D_ML), lambda bi, i: (bi * nc + i, 2)),
            pl.BlockSpec((L, D_ML), lambda bi, i: (bi * nc + i, 3)),
            pl.BlockSpec((L, 2 * N_HEADS_ML), lambda bi, i: (bi * nc + i, 0)),
            pl.BlockSpec((None, 2 * N_HEADS_ML, L), lambda bi, i: (bi, 0, i)),
            lw((CONV_W, 2 * D_ML)), lw((1, 2 * N_HEADS_ML)), lw((2 * N_HEADS_ML, 1)),
        ],
        out_specs=pl.BlockSpec((L, D_ML), lambda bi, i: (bi * nc + i, 0)),
        out_shape=jax.ShapeDtypeStruct((b * s, D_ML), BF16),
        scratch_shapes=[
            pltpu.VMEM((L + CONV_HALO, 2 * D_ML), F32),
            pltpu.VMEM((N_HEADS_ML, HD_ML, HD_ML), F32),
            pltpu.VMEM((N_HEADS_ML, 8, HD_ML), F32),
            pltpu.VMEM((N_HEADS_ML, 8, LANES), F32),
        ],
        compiler_params=_cparams(("parallel", "arbitrary")),
        name="mlstm",
    )(ml, ml, ml, gcol, grow, convw, bcol, brow)


def _outproj_body(x_ref, ya_ref, yml_ref, wout_ref, g_ref, wr_ref, br_ref,
                  xnew_ref, hn_ref, ri_ref, rg_ref, cnt_ref):
    tm = x_ref.shape[0]

    @pl.when(pl.program_id(0) == 0)
    def _():
        cnt_ref[...] = jnp.zeros_like(cnt_ref)

    y = jnp.concatenate([ya_ref[p] for p in range(N_PAIRS)] + [yml_ref[...]], axis=-1)
    xnew = x_ref[...] + jnp.dot(y, wout_ref[...], preferred_element_type=F32)
    xnew_ref[...] = xnew
    ms = jnp.mean(xnew * xnew, axis=-1, keepdims=True)
    hn = xnew * lax.rsqrt(ms + EPS) * g_ref[...]
    hn_ref[...] = hn.astype(hn_ref.dtype)

    hn_hi = hn.astype(BF16)
    hn_lo = (hn - hn_hi.astype(F32)).astype(BF16)
    w_hi, w_lo = wr_ref[0], wr_ref[1]
    logits = (jnp.dot(hn_hi, w_hi, preferred_element_type=F32)
              + (jnp.dot(hn_lo, w_hi, preferred_element_type=F32)
                 + jnp.dot(hn_hi, w_lo, preferred_element_type=F32))) + br_ref[...]
    lane = lax.broadcasted_iota(jnp.int32, (tm, LANES), 1)
    lane_f = lane.astype(F32)
    big = float(LANES)

    def first_max(vals, valid):
        masked = jnp.where(valid, vals, -jnp.inf)
        top = jnp.max(masked, axis=-1, keepdims=True)
        idx = jnp.min(jnp.where(valid & (masked == top), lane_f, big), axis=-1, keepdims=True)
        return top, idx

    is_group = lane < N_GROUPS
    g_top, g_sel = first_max(logits, is_group)
    g_gate = 1.0 / jnp.sum(jnp.where(is_group, jnp.exp(logits - g_top), 0.0), axis=-1, keepdims=True)
    lo = N_GROUPS + EXPERTS_PER_GROUP * g_sel
    in_group = (lane_f >= lo) & (lane_f < lo + EXPERTS_PER_GROUP)
    t1, i1 = first_max(logits, in_group)
    t2, i2 = first_max(logits, in_group & (lane_f != i1))
    r = jnp.exp(t2 - t1)
    p1 = 1.0 / (1.0 + r)
    e1 = i1 - N_GROUPS
    e2 = i2 - N_GROUPS

    hit1 = lane_f == e1
    hit2 = lane_f == e2
    onehot = jnp.where(hit1 | hit2, 1.0, 0.0)
    ti = lax.broadcasted_iota(jnp.int32, (tm, tm), 0)
    tj = lax.broadcasted_iota(jnp.int32, (tm, tm), 1)
    strict_lower = jnp.where(tj < ti, 1.0, 0.0).astype(BF16)
    before = jnp.dot(strict_lower, onehot.astype(BF16), preferred_element_type=F32) + cnt_ref[...]
    rank1 = jnp.sum(jnp.where(hit1, before, 0.0), axis=-1, keepdims=True)
    rank2 = jnp.sum(jnp.where(hit2, before, 0.0), axis=-1, keepdims=True)
    cnt_ref[...] = cnt_ref[...] + jnp.sum(onehot, axis=0, keepdims=True)

    ri = jnp.where(lane == 0, e1, jnp.where(lane == 1, e2, jnp.where(lane == 2, rank1,
                   jnp.where(lane == 3, rank2, 0.0))))
    ri_ref[...] = ri.astype(jnp.int32)
    rg_ref[...] = jnp.where(lane == 0, g_gate * p1, jnp.where(lane == 1, g_gate * r * p1, 0.0))


def _outproj_router(xf, ya, yml, wout, g, wr, br, layer, b, s, tm=512):
    t, d = xf.shape
    ns = s // tm
    lw = lambda shape: pl.BlockSpec((None,) + shape, lambda i: (layer,) + (0,) * len(shape))
    row = lambda width: pl.BlockSpec((tm, width), lambda i: (i, 0))
    return pl.pallas_call(
        _outproj_body,
        grid=(t // tm,),
        in_specs=[
            row(d),
            pl.BlockSpec((None, N_PAIRS, tm, LANES), lambda i: (i // ns, 0, i % ns, 0)),
            row(D_ML),
            lw((d, d)), lw((1, d)), lw((2, d, LANES)), lw((1, LANES)),
        ],
        out_specs=[row(d), row(d), row(LANES), row(LANES), pl.BlockSpec((1, LANES), lambda i: (0, 0))],
        out_shape=[
            jax.ShapeDtypeStruct((t, d), F32),
            jax.ShapeDtypeStruct((t, d), BF16),
            jax.ShapeDtypeStruct((t, LANES), jnp.int32),
            jax.ShapeDtypeStruct((t, LANES), F32),
            jax.ShapeDtypeStruct((1, LANES), F32),
        ],
        compiler_params=_cparams(("arbitrary",)),
        name="outproj_router",
    )(xf, ya, yml, wout, g, wr, br)


def _expert_body(be_ref, bv_ref, xs_ref, w1_ref, w3_ref, w2_ref, yb_ref, w1b, w3b, w2b):
    i = pl.program_id(0)
    e = be_ref[i]
    e_prev = be_ref[jnp.maximum(i - 1, 0)]
    valid = bv_ref[i]

    @pl.when((i == 0) | (e != e_prev))
    def _():
        w1b[...] = w1_ref[...].astype(BF16)
        w3b[...] = w3_ref[...].astype(BF16)
        w2b[...] = w2_ref[...].astype(BF16)

    @pl.when(valid > 0)
    def _():
        x = xs_ref[...]
        a = jnp.dot(x, w1b[...], preferred_element_type=F32)
        g = jnp.dot(x, w3b[...], preferred_element_type=F32)
        hdn = (a * jax.nn.sigmoid(a) * g).astype(BF16)
        yb_ref[...] = jnp.dot(hdn, w2b[...], preferred_element_type=F32).astype(yb_ref.dtype)

    @pl.when(valid == 0)
    def _():
        yb_ref[...] = jnp.zeros_like(yb_ref)


def _experts(xs, blk_expert, blk_valid, w1, w3, w2, layer):
    cap, d = xs.shape
    n_blk = cap // MOE_BLOCK
    grid_spec = pltpu.PrefetchScalarGridSpec(
        num_scalar_prefetch=2,
        grid=(n_blk,),
        in_specs=[
            pl.BlockSpec((MOE_BLOCK, d), lambda i, be, nu: (i, 0)),
            pl.BlockSpec((None, None, d, D_EXPERT), lambda i, be, nu: (layer, be[i], 0, 0)),
            pl.BlockSpec((None, None, d, D_EXPERT), lambda i, be, nu: (layer, be[i], 0, 0)),
            pl.BlockSpec((None, None, D_EXPERT, d), lambda i, be, nu: (layer, be[i], 0, 0)),
        ],
        out_specs=pl.BlockSpec((MOE_BLOCK, d), lambda i, be, nu: (i, 0)),
        scratch_shapes=[
            pltpu.VMEM((d, D_EXPERT), BF16),
            pltpu.VMEM((d, D_EXPERT), BF16),
            pltpu.VMEM((D_EXPERT, d), BF16),
        ],
    )
    return pl.pallas_call(
        _expert_body,
        grid_spec=grid_spec,
        out_shape=jax.ShapeDtypeStruct((cap, d), BF16),
        compiler_params=_cparams(("arbitrary",)),
        name="experts",
    )(blk_expert, blk_valid, xs, w1, w3, w2)


def _combine_body(x_ref, y0_ref, y1_ref, rg_ref, o_ref):
    g = rg_ref[...]
    o_ref[...] = (x_ref[...] + g[:, 0:1] * y0_ref[...].astype(F32)
                  + g[:, 1:2] * y1_ref[...].astype(F32))


def _combine(xnew, y0, y1, rg, tm=1024):
    t, d = xnew.shape
    row = lambda width: pl.BlockSpec((tm, width), lambda i: (i, 0))
    return pl.pallas_call(
        _combine_body,
        grid=(t // tm,),
        in_specs=[row(d), row(d), row(d), row(LANES)],
        out_specs=row(d),
        out_shape=jax.ShapeDtypeStruct((t, d), F32),
        compiler_params=_cparams(("parallel",)),
        name="combine",
    )(xnew, y0, y1, rg)


def _moe_dispatch_plan(ri, cnt, t):
    n_blk = (t * TOP_K) // MOE_BLOCK + N_EXPERTS
    counts = cnt[0, :N_EXPERTS].astype(jnp.int32)
    padded = (counts + MOE_BLOCK - 1) // MOE_BLOCK * MOE_BLOCK
    pend = jnp.cumsum(padded)
    pstart = pend - padded
    dest = pstart[ri[:, 0:2]] + ri[:, 2:4]
    blk_start = jnp.arange(n_blk, dtype=jnp.int32) * MOE_BLOCK
    blk_expert = jnp.minimum(jnp.sum((pend[None, :] <= blk_start[:, None]).astype(jnp.int32), axis=1),
                             N_EXPERTS - 1)
    blk_valid = jnp.clip((pstart + counts)[blk_expert] - blk_start, 0, MOE_BLOCK)
    return dest, blk_expert, blk_valid, n_blk


def _layer(xf, moe, layer, p, b, s):
    t = b * s
    x, q, k, v, ml, gcol, grow = _inproj(xf, moe, p, layer, b, s)
    ya = _attention(q, k, v, p["logit_bound"][layer])
    yml = _mlstm(ml, gcol, grow, p["conv_w"], p["b_col"], p["b_row"], layer, b, s)
    xnew, hn, ri, rg, cnt = _outproj_router(x, ya, yml, p["w_out"], p["g_ffn"], p["w_r"], p["b_r"], layer, b, s)
    dest, blk_expert, blk_valid, n_blk = _moe_dispatch_plan(ri, cnt, t)
    tok = jnp.broadcast_to(jnp.arange(t, dtype=jnp.int32)[:, None], (t, TOP_K))
    buf_tok = jnp.zeros((n_blk * MOE_BLOCK,), jnp.int32).at[dest.reshape(-1)].set(tok.reshape(-1))
    xs = hn[buf_tok]
    yb = _experts(xs, blk_expert, blk_valid, p["w1"], p["w3"], p["w2"], layer)
    return xnew, (yb[dest[:, 0]], yb[dest[:, 1]], rg)


def _prep(g_norm_mix, w_in, b_igate, b_fgate, conv_w, g_q, g_k, w_out, g_norm_ffn,
          w_group, b_group, w_expert_router, b_expert_router, w1, w3, w2):
    w_in_b = w_in.astype(BF16)
    n_gate = 2 * N_HEADS_ML
    hd_id = jnp.arange(D_ATT, dtype=jnp.int32) // HD_ATT
    pad_r = LANES - N_GROUPS - N_EXPERTS
    w_r = jnp.pad(jnp.concatenate([w_group, w_expert_router], axis=-1), ((0, 0), (0, 0), (0, pad_r)))
    w_r_hi = w_r.astype(BF16)
    w_r_lo = (w_r - w_r_hi.astype(F32)).astype(BF16)
    p = {
        "g_mix": g_norm_mix[:, None, :],
        "w_att": w_in_b[:, :, :3 * D_ATT],
        "w_ml": w_in_b[:, :, 3 * D_ATT:3 * D_ATT + 4 * D_ML],
        "w_g": w_in_b[:, :, -n_gate:],
        "w_gt": jnp.swapaxes(w_in_b[:, :, -n_gate:], 1, 2),
        "bd": jnp.where(hd_id[:, None] == hd_id[None, :], 1.0 / HD_ATT, 0.0).astype(BF16),
        "gq": (jnp.tile(g_q, (1, N_HEADS_ATT)) * (HD_ATT ** -0.5))[:, None, :],
        "gk": jnp.tile(g_k, (1, N_HEADS_ATT))[:, None, :],
        "logit_bound": (1.02 * HD_ATT ** 0.5 * jnp.max(jnp.abs(g_q), axis=-1)
                        * jnp.max(jnp.abs(g_k), axis=-1))[:, None],
        "conv_w": conv_w,
        "b_col": jnp.concatenate([b_igate, b_fgate], axis=-1)[:, None, :],
        "b_row": jnp.concatenate([b_igate, b_fgate], axis=-1)[:, :, None],
        "w_out": w_out.astype(BF16),
        "g_ffn": g_norm_ffn[:, None, :],
        "w_r": jnp.stack([w_r_hi, w_r_lo], axis=1),
        "b_r": jnp.pad(jnp.concatenate([b_group, b_expert_router], axis=-1), ((0, 0), (0, pad_r)))[:, None, :],
        "w1": w1, "w3": w3, "w2": w2,
    }
    return p


def kernel(x, g_norm_mix, w_in, b_igate, b_fgate, conv_w, g_q, g_k, w_out, g_norm_ffn,
           w_group, b_group, w_expert_router, b_expert_router, w1, w3, w2):
    p = _prep(g_norm_mix, w_in, b_igate, b_fgate, conv_w, g_q, g_k, w_out, g_norm_ffn,
              w_group, b_group, w_expert_router, b_expert_router, w1, w3, w2)
    b, s, d = x.shape
    xf, moe = x.reshape(b * s, d), None
    for layer in range(w_in.shape[0]):
        xf, moe = _layer(xf, moe, layer, p, b, s)
    return _combine(xf, *moe).reshape(b, s, d)
```

```python
import functools

import jax
import jax.numpy as jnp
from jax import lax
from jax.experimental import pallas as pl
from jax.experimental.pallas import tpu as pltpu

F32 = jnp.float32
BF16 = jnp.bfloat16

EPS = 1e-6
D_MODEL = 1024
N_HEADS_ATT = 8
HD_ATT = 64
D_ATT = N_HEADS_ATT * HD_ATT
N_PAIRS = D_ATT // 128
DILATIONS = (1, 4, 16)
BAND = 128
ATT_TILE = BAND * 16
N_HEADS_ML = 4
HD_ML = 128
D_ML = N_HEADS_ML * HD_ML
CONV_W = 4
CONV_HALO = 16
ML_CHUNK = 256
N_GROUPS = 4
EXPERTS_PER_GROUP = 8
N_EXPERTS = N_GROUPS * EXPERTS_PER_GROUP
TOP_K = 2
D_EXPERT = 512
MOE_BLOCK = 256
MOE_PARTS = 4
LANES = 128
ROUTE_W = 8
NEG = -1e30
VMEM_LIMIT = 56 * 1024 * 1024


def _cparams(sem):
    return pltpu.CompilerParams(dimension_semantics=sem, vmem_limit_bytes=VMEM_LIMIT)


def _inproj_body(*refs, fused_combine):
    if fused_combine:
        (x_ref, y0_ref, y1_ref, rg_ref, g_ref, watt_ref, wml_ref, wg_ref, wgt_ref, bd_ref, gq_ref, gk_ref,
         xo_ref, q_ref, k_ref, v_ref, ml_ref, gcol_ref, grow_ref) = refs
        rg = rg_ref[...]
        x = (x_ref[...] + rg[:, 0:1] * y0_ref[...].astype(F32) + rg[:, 1:2] * y1_ref[...].astype(F32))
        xo_ref[...] = x
    else:
        (x_ref, g_ref, watt_ref, wml_ref, wg_ref, wgt_ref, bd_ref, gq_ref, gk_ref,
         q_ref, k_ref, v_ref, ml_ref, gcol_ref, grow_ref) = refs
        x = x_ref[...]
    ms = jnp.mean(x * x, axis=-1, keepdims=True)
    xn = (x * lax.rsqrt(ms + EPS) * g_ref[...]).astype(BF16)
    att = jnp.dot(xn, watt_ref[...], preferred_element_type=F32)

    def headnorm(t, gain):
        msq = jnp.dot((t * t).astype(BF16), bd_ref[...], preferred_element_type=F32)
        return t * lax.rsqrt(msq + EPS) * gain

    q = headnorm(att[:, :D_ATT], gq_ref[...])
    k = headnorm(att[:, D_ATT:2 * D_ATT], gk_ref[...])
    v = att[:, 2 * D_ATT:]
    for p in range(N_PAIRS):
        sl = slice(LANES * p, LANES * (p + 1))
        q_ref[p] = q[:, sl].astype(BF16)
        k_ref[p] = k[:, sl].astype(BF16)
        v_ref[p] = v[:, sl].astype(BF16)
    ml_ref[...] = jnp.dot(xn, wml_ref[...], preferred_element_type=F32).astype(BF16)
    gcol_ref[...] = jnp.dot(xn, wg_ref[...], preferred_element_type=F32)
    grow_ref[...] = lax.dot_general(wgt_ref[...], xn, (((1,), (1,)), ((), ())),
                                    preferred_element_type=F32)


def _inproj(xf, moe, p, layer, b, s, tm=512):
    d = xf.shape[1]
    ns = s // tm
    pair_spec = pl.BlockSpec((None, N_PAIRS, tm, LANES), lambda i: (i // ns, 0, i % ns, 0))
    pair_shape = jax.ShapeDtypeStruct((b, N_PAIRS, s, LANES), BF16)
    lw = lambda shape: pl.BlockSpec((None,) + shape, lambda i: (layer,) + (0,) * len(shape))
    row = lambda width: pl.BlockSpec((tm, width), lambda i: (i, 0))
    in_specs = [
        lw((1, d)), lw((d, 3 * D_ATT)), lw((d, 4 * D_ML)), lw((d, 2 * N_HEADS_ML)),
        lw((2 * N_HEADS_ML, d)),
        pl.BlockSpec((D_ATT, D_ATT), lambda i: (0, 0)),
        lw((1, D_ATT)), lw((1, D_ATT)),
    ]
    out_specs = [
        pair_spec, pair_spec, pair_spec,
        row(4 * D_ML), row(2 * N_HEADS_ML),
        pl.BlockSpec((None, 2 * N_HEADS_ML, tm), lambda i: (i // ns, 0, i % ns)),
    ]
    out_shape = [
        pair_shape, pair_shape, pair_shape,
        jax.ShapeDtypeStruct((b * s, 4 * D_ML), BF16),
        jax.ShapeDtypeStruct((b * s, 2 * N_HEADS_ML), F32),
        jax.ShapeDtypeStruct((b, 2 * N_HEADS_ML, s), F32),
    ]
    args = [p["g_mix"], p["w_att"], p["w_ml"], p["w_g"], p["w_gt"], p["bd"], p["gq"], p["gk"]]
    if moe is None:
        in_specs = [row(d)] + in_specs
        args = [xf] + args
    else:
        in_specs = [row(d), row(d), row(d), row(ROUTE_W)] + in_specs
        args = [xf, *moe] + args
        out_specs = [row(d)] + out_specs
        out_shape = [jax.ShapeDtypeStruct((b * s, d), F32)] + out_shape
    outs = pl.pallas_call(
        functools.partial(_inproj_body, fused_combine=moe is not None),
        grid=(b * ns,),
        in_specs=in_specs,
        out_specs=out_specs,
        out_shape=out_shape,
        compiler_params=_cparams(("parallel",)),
        name="inproj",
    )(*args)
    return outs if moe is not None else [xf] + list(outs)


ATT_SAFE_LOGIT_BOUND = 40.0


def _attn_body(bound_ref, q_ref, k_ref, v_ref, kp_ref, vp_ref, o_ref, qf, kf, vf, acc_ref, m_ref, l_ref,
               *, fixed_shift):
    tile = pl.program_id(2)
    lane = lax.broadcasted_iota(jnp.int32, (1, LANES), 1)
    first_head = lane < HD_ATT
    qi = lax.broadcasted_iota(jnp.int32, (BAND, 2 * BAND), 0)
    kj = lax.broadcasted_iota(jnp.int32, (BAND, 2 * BAND), 1)
    band = (kj >= qi) & (kj <= qi + BAND)
    shift = -bound_ref[0] if fixed_shift else 0.0
    bias_prev = jnp.where(band, shift, NEG).astype(F32)
    bias_first = jnp.where(band & ((kj >= BAND) | (tile > 0)), shift, NEG).astype(F32)

    qf[...] = q_ref[...].astype(F32)
    kf[0:ATT_TILE, :] = kp_ref[...].astype(F32)
    kf[ATT_TILE:, :] = k_ref[...].astype(F32)
    vf[0:ATT_TILE, :] = vp_ref[...].astype(F32)
    vf[ATT_TILE:, :] = v_ref[...].astype(F32)
    ones = jnp.ones((2 * BAND, LANES), BF16)

    def unit(d, r, n, mode):
        q0 = n * BAND * d + r
        k0 = ATT_TILE + (n - 1) * BAND * d + r
        rows = pl.ds(q0, BAND, stride=d) if d > 1 else pl.ds(q0, BAND)
        krows = pl.ds(k0, 2 * BAND, stride=d) if d > 1 else pl.ds(k0, 2 * BAND)
        q = qf[rows, :].astype(BF16)
        kcat = kf[krows, :].astype(BF16)
        vcat = vf[krows, :].astype(BF16)
        bias = bias_first if n == 0 else bias_prev
        if fixed_shift:
            vones = jnp.concatenate([vcat, ones], axis=1)
            outs = []
            for h in range(2):
                qh = jnp.where(first_head if h == 0 else ~first_head, q, jnp.zeros_like(q))
                s = lax.dot_general(qh, kcat, (((1,), (1,)), ((), ())), preferred_element_type=F32) + bias
                outs.append(jnp.dot(jnp.exp(s).astype(BF16), vones, preferred_element_type=F32))
            acc = jnp.where(first_head, outs[0][:, :LANES], outs[1][:, :LANES])
            l_pair = jnp.where(first_head, outs[0][:, LANES:], outs[1][:, LANES:])
            if mode != "init":
                acc = acc + acc_ref[rows, :]
                l_pair = l_pair + l_ref[rows, :]
            if mode == "final":
                o_ref[rows, :] = (acc / l_pair).astype(o_ref.dtype)
            else:
                l_ref[rows, :] = l_pair
                acc_ref[rows, :] = acc
            return
        if mode != "init":
            m_old = m_ref[rows, :]
            l_old = l_ref[rows, :]
        stats = []
        for h in range(2):
            qh = jnp.where(first_head if h == 0 else ~first_head, q, jnp.zeros_like(q))
            s = lax.dot_general(qh, kcat, (((1,), (1,)), ((), ())), preferred_element_type=F32) + bias
            m_new = jnp.max(s, axis=-1, keepdims=True)
            if mode != "init":
                m_new = jnp.maximum(m_new, m_old[:, h * HD_ATT:h * HD_ATT + 1])
            p = jnp.exp(s - m_new)
            l_new = jnp.sum(p, axis=-1, keepdims=True)
            pv = jnp.dot(p.astype(BF16), vcat, preferred_element_type=F32)
            stats.append((m_new, l_new, pv))
        (m0, l0, pv0), (m1, l1, pv1) = stats
        m_pair = jnp.where(first_head, m0, m1)
        l_pair = jnp.where(first_head, l0, l1)
        acc = jnp.where(first_head, pv0, pv1)
        if mode != "init":
            alpha = jnp.exp(m_old - m_pair)
            l_pair = l_pair + alpha * l_old
            acc = acc + alpha * acc_ref[rows, :]
        if mode == "final":
            o_ref[rows, :] = (acc / l_pair).astype(o_ref.dtype)
        else:
            m_ref[rows, :] = m_pair
            l_ref[rows, :] = l_pair
            acc_ref[rows, :] = acc

    for d, mode in ((16, "init"), (4, "update"), (1, "final")):
        for r in range(d):
            for n in range(ATT_TILE // (BAND * d)):
                unit(d, r, n, mode)


def _attention(q, k, v, logit_bound):
    b, npair, s, _ = q.shape
    cur = pl.BlockSpec((None, None, ATT_TILE, LANES), lambda bi, p, i: (bi, p, i, 0))
    prev = pl.BlockSpec((None, None, ATT_TILE, LANES), lambda bi, p, i: (bi, p, jnp.maximum(i - 1, 0), 0))

    def run(fixed_shift):
        return pl.pallas_call(
            functools.partial(_attn_body, fixed_shift=fixed_shift),
            grid=(b, npair, s // ATT_TILE),
            in_specs=[pl.BlockSpec(memory_space=pltpu.SMEM), cur, cur, cur, prev, prev],
            out_specs=cur,
            out_shape=jax.ShapeDtypeStruct((b, npair, s, LANES), BF16),
            scratch_shapes=[
                pltpu.VMEM((ATT_TILE, LANES), F32),
                pltpu.VMEM((2 * ATT_TILE, LANES), F32),
                pltpu.VMEM((2 * ATT_TILE, LANES), F32),
                pltpu.VMEM((ATT_TILE, LANES), F32),
                pltpu.VMEM((ATT_TILE, LANES), F32),
                pltpu.VMEM((ATT_TILE, LANES), F32),
            ],
            compiler_params=_cparams(("parallel", "parallel", "parallel")),
            name="dilated_attn_fixed_shift" if fixed_shift else "dilated_attn",
        )(logit_bound, q, k, v, k, v)

    return lax.cond(logit_bound[0] <= ATT_SAFE_LOGIT_BOUND, lambda: run(True), lambda: run(False))


def _log_sigmoid(z):
    return jnp.minimum(z, 0.0) - jnp.log(1.0 + jnp.exp(-jnp.abs(z)))


def _mlstm_body(qk_ref, v_ref, o_ref, gcol_ref, grow_ref, convw_ref, bcol_ref, brow_ref,
                y_ref, xc_ref, c_ref, n_ref, m_ref):
    L = ML_CHUNK
    chunk = pl.program_id(1)

    @pl.when(chunk == 0)
    def _():
        xc_ref[0:CONV_HALO, :] = jnp.zeros((CONV_HALO, 2 * D_ML), F32)
        c_ref[...] = jnp.zeros_like(c_ref)
        n_ref[...] = jnp.zeros_like(n_ref)
        m_ref[...] = jnp.zeros_like(m_ref)

    xc_ref[CONV_HALO:CONV_HALO + L, :] = qk_ref[...].astype(F32)
    w = convw_ref[...]
    conv = jnp.zeros((L, 2 * D_ML), F32)
    for j in range(CONV_W):
        off = CONV_HALO - (CONV_W - 1) + j
        conv = conv + w[j:j + 1, :] * xc_ref[off:off + L, :]
    xc_ref[0:CONV_HALO, :] = xc_ref[L:L + CONV_HALO, :]
    qkc = conv * jax.nn.sigmoid(conv)

    gc = gcol_ref[...] + bcol_ref[...]
    gr = grow_ref[...] + brow_ref[...]
    i_col, f_col = gc[:, :N_HEADS_ML], _log_sigmoid(gc[:, N_HEADS_ML:])
    i_row, f_row = gr[:N_HEADS_ML, :], _log_sigmoid(gr[N_HEADS_ML:, :])
    t_idx = lax.broadcasted_iota(jnp.int32, (L, L), 0)
    s_idx = lax.broadcasted_iota(jnp.int32, (L, L), 1)
    causal = s_idx <= t_idx

    for h in range(N_HEADS_ML):
        ln = slice(h * HD_ML, (h + 1) * HD_ML)
        q = qkc[:, ln]
        k = qkc[:, D_ML + h * HD_ML:D_ML + (h + 1) * HD_ML] * (HD_ML ** -0.5)
        v = v_ref[:, ln]
        qb, kb = q.astype(BF16), k.astype(BF16)
        b_col = jnp.sum(jnp.where(causal, f_row[h:h + 1, :], 0.0), axis=1, keepdims=True)
        b_row = jnp.sum(jnp.where(t_idx <= s_idx, f_col[:, h:h + 1], 0.0), axis=0, keepdims=True)
        a_row = i_row[h:h + 1, :] - b_row
        a_col = i_col[:, h:h + 1] - b_col
        m_st = m_ref[h][0:1, 0:1]
        a_mat = jnp.where(causal, a_row, -jnp.inf)
        m_col = jnp.maximum(jnp.max(a_mat, axis=1, keepdims=True), m_st)
        w_intra = jnp.exp(a_mat - m_col)
        w_inter = jnp.exp(m_st - m_col)
        qk = lax.dot_general(qb, kb, (((1,), (1,)), ((), ())), preferred_element_type=F32) * w_intra
        c_st = c_ref[h]
        n_st = n_ref[h][0:1, :]
        num = (jnp.dot(qk.astype(BF16), v, preferred_element_type=F32)
               + w_inter * jnp.dot(qb, c_st.astype(BF16), preferred_element_type=F32))
        den = (jnp.sum(qk, axis=1, keepdims=True)
               + w_inter * jnp.sum(q * n_st, axis=1, keepdims=True))
        h_out = num / jnp.maximum(jnp.abs(den), jnp.exp(-(b_col + m_col)))
        m_last = m_col[L - 1:L, :]
        b_last = b_col[L - 1:L, :]
        w_s = jnp.exp(a_col - m_last)
        w_c = jnp.exp(m_st - m_last)
        vs = (w_s * v.astype(F32)).astype(BF16)
        c_ref[h] = w_c * c_st + lax.dot_general(kb, vs, (((0,), (0,)), ((), ())),
                                                preferred_element_type=F32)
        n_new = w_c * n_st + jnp.sum(w_s * k, axis=0, keepdims=True)
        n_ref[h] = jnp.broadcast_to(n_new, (8, HD_ML))
        m_ref[h] = jnp.broadcast_to(b_last + m_last, (8, LANES))
        y_ref[:, ln] = (jax.nn.sigmoid(o_ref[:, ln].astype(F32)) * h_out).astype(y_ref.dtype)


def _mlstm(ml, gcol, grow, convw, bcol, brow, layer, b, s):
    L = ML_CHUNK
    nc = s // L
    lw = lambda shape: pl.BlockSpec((None,) + shape, lambda bi, i: (layer,) + (0,) * len(shape))
    return pl.pallas_call(
        _mlstm_body,
        grid=(b, nc),
        in_specs=[
            pl.BlockSpec((L, 2 * D_ML), lambda bi, i: (bi * nc + i, 0)),
            pl.BlockSpec((L, D_ML), lambda bi, i: (bi * nc + i, 2)),
            pl.BlockSpec((L, D_ML), lambda bi, i: (bi * nc + i, 3)),
            pl.BlockSpec((L, 2 * N_HEADS_ML), lambda bi, i: (bi * nc + i, 0)),
            pl.BlockSpec((None, 2 * N_HEADS_ML, L), lambda bi, i: (bi, 0, i)),
            lw((CONV_W, 2 * D_ML)), lw((1, 2 * N_HEADS_ML)), lw((2 * N_HEADS_ML, 1)),
        ],
        out_specs=pl.BlockSpec((L, D_ML), lambda bi, i: (bi * nc + i, 0)),
        out_shape=jax.ShapeDtypeStruct((b * s, D_ML), BF16),
        scratch_shapes=[
            pltpu.VMEM((L + CONV_HALO, 2 * D_ML), F32),
            pltpu.VMEM((N_HEADS_ML, HD_ML, HD_ML), F32),
            pltpu.VMEM((N_HEADS_ML, 8, HD_ML), F32),
            pltpu.VMEM((N_HEADS_ML, 8, LANES), F32),
        ],
        compiler_params=_cparams(("parallel", "arbitrary")),
        name="mlstm",
    )(ml, ml, ml, gcol, grow, convw, bcol, brow)


def _outproj_body(x_ref, ya_ref, yml_ref, wout_ref, g_ref, wr_ref, br_ref,
                  xnew_ref, hn_ref, ri_ref, rg_ref, cnt_ref):
    tm = x_ref.shape[0]

    @pl.when(pl.program_id(0) == 0)
    def _():
        cnt_ref[...] = jnp.zeros_like(cnt_ref)

    y = jnp.concatenate([ya_ref[p] for p in range(N_PAIRS)] + [yml_ref[...]], axis=-1)
    xnew = x_ref[...] + jnp.dot(y, wout_ref[...], preferred_element_type=F32)
    xnew_ref[...] = xnew
    ms = jnp.mean(xnew * xnew, axis=-1, keepdims=True)
    hn = xnew * lax.rsqrt(ms + EPS) * g_ref[...]
    hn_ref[...] = hn.astype(hn_ref.dtype)

    hn_hi = hn.astype(BF16)
    hn_lo = (hn - hn_hi.astype(F32)).astype(BF16)
    hi_prod = jnp.dot(hn_hi, wr_ref[...], preferred_element_type=F32)
    logits = (hi_prod[:, :LANES]
              + (jnp.dot(hn_lo, wr_ref[:, :LANES], preferred_element_type=F32) + hi_prod[:, LANES:])
              ) + br_ref[...]
    lane = lax.broadcasted_iota(jnp.int32, (tm, LANES), 1)
    lane_f = lane.astype(F32)
    big = float(LANES)

    def first_max(vals, valid):
        masked = jnp.where(valid, vals, -jnp.inf)
        top = jnp.max(masked, axis=-1, keepdims=True)
        idx = jnp.min(jnp.where(valid & (masked == top), lane_f, big), axis=-1, keepdims=True)
        return top, idx

    is_group = lane < N_GROUPS
    g_top, g_sel = first_max(logits, is_group)
    g_gate = 1.0 / jnp.sum(jnp.where(is_group, jnp.exp(logits - g_top), 0.0), axis=-1, keepdims=True)
    lo = N_GROUPS + EXPERTS_PER_GROUP * g_sel
    in_group = (lane_f >= lo) & (lane_f < lo + EXPERTS_PER_GROUP)
    t1, i1 = first_max(logits, in_group)
    t2, i2 = first_max(logits, in_group & (lane_f != i1))
    r = jnp.exp(t2 - t1)
    p1 = 1.0 / (1.0 + r)
    e1 = i1 - N_GROUPS
    e2 = i2 - N_GROUPS

    hit1 = lane_f == e1
    hit2 = lane_f == e2
    onehot = jnp.where(hit1 | hit2, 1.0, 0.0)
    ti = lax.broadcasted_iota(jnp.int32, (tm, tm), 0)
    tj = lax.broadcasted_iota(jnp.int32, (tm, tm), 1)
    strict_lower = jnp.where(tj < ti, 1.0, 0.0).astype(BF16)
    before = jnp.dot(strict_lower, onehot.astype(BF16), preferred_element_type=F32) + cnt_ref[...]
    rank1 = jnp.sum(jnp.where(hit1, before, 0.0), axis=-1, keepdims=True)
    rank2 = jnp.sum(jnp.where(hit2, before, 0.0), axis=-1, keepdims=True)
    cnt_ref[...] = cnt_ref[...] + jnp.sum(onehot, axis=0, keepdims=True)

    ri = jnp.where(lane == 0, e1, jnp.where(lane == 1, e2, jnp.where(lane == 2, rank1,
                   jnp.where(lane == 3, rank2, 0.0))))
    rg = jnp.where(lane == 0, g_gate * p1, jnp.where(lane == 1, g_gate * r * p1, 0.0))
    ri_ref[...] = ri[:, :ROUTE_W].astype(jnp.int32)
    rg_ref[...] = rg[:, :ROUTE_W]


def _outproj_router(xf, ya, yml, wout, g, wr, br, layer, b, s, tm=512):
    t, d = xf.shape
    ns = s // tm
    lw = lambda shape: pl.BlockSpec((None,) + shape, lambda i: (layer,) + (0,) * len(shape))
    row = lambda width: pl.BlockSpec((tm, width), lambda i: (i, 0))
    return pl.pallas_call(
        _outproj_body,
        grid=(t // tm,),
        in_specs=[
            row(d),
            pl.BlockSpec((None, N_PAIRS, tm, LANES), lambda i: (i // ns, 0, i % ns, 0)),
            row(D_ML),
            lw((d, d)), lw((1, d)), lw((d, 2 * LANES)), lw((1, LANES)),
        ],
        out_specs=[row(d), row(d), row(ROUTE_W), row(ROUTE_W), pl.BlockSpec((1, LANES), lambda i: (0, 0))],
        out_shape=[
            jax.ShapeDtypeStruct((t, d), F32),
            jax.ShapeDtypeStruct((t, d), BF16),
            jax.ShapeDtypeStruct((t, ROUTE_W), jnp.int32),
            jax.ShapeDtypeStruct((t, ROUTE_W), F32),
            jax.ShapeDtypeStruct((1, LANES), F32),
        ],
        compiler_params=_cparams(("arbitrary",)),
        name="outproj_router",
    )(xf, ya, yml, wout, g, wr, br)


def _expert_body(be_ref, bv_ref, xs_ref, w1_ref, w3_ref, w2_ref, yb_ref, w1b, w3b, w2b):
    i = pl.program_id(0)
    e = be_ref[i]
    e_prev = be_ref[jnp.maximum(i - 1, 0)]
    valid = bv_ref[i]

    @pl.when((i == 0) | (e != e_prev))
    def _():
        w1b[...] = w1_ref[...].astype(BF16)
        w3b[...] = w3_ref[...].astype(BF16)
        w2b[...] = w2_ref[...].astype(BF16)

    @pl.when(valid > 0)
    def _():
        x = xs_ref[...]
        a = jnp.dot(x, w1b[...], preferred_element_type=F32)
        g = jnp.dot(x, w3b[...], preferred_element_type=F32)
        hdn = (a * jax.nn.sigmoid(a) * g).astype(BF16)
        yb_ref[...] = jnp.dot(hdn, w2b[...], preferred_element_type=F32).astype(yb_ref.dtype)

    @pl.when(valid == 0)
    def _():
        yb_ref[...] = jnp.zeros_like(yb_ref)


def _experts(xs, blk_expert, blk_valid, w1, w3, w2, layer, blk0, n_blk_total, y_so_far=None):
    rows, d = xs.shape
    n_blk = rows // MOE_BLOCK
    in_specs = [
        pl.BlockSpec((MOE_BLOCK, d), lambda i, be, nu: (i, 0)),
        pl.BlockSpec((None, None, d, D_EXPERT), lambda i, be, nu: (layer, be[i], 0, 0)),
        pl.BlockSpec((None, None, d, D_EXPERT), lambda i, be, nu: (layer, be[i], 0, 0)),
        pl.BlockSpec((None, None, D_EXPERT, d), lambda i, be, nu: (layer, be[i], 0, 0)),
    ]
    args = [blk_expert, blk_valid, xs, w1, w3, w2]
    aliases = {}
    body = _expert_body
    if y_so_far is not None:
        in_specs.append(pl.BlockSpec(memory_space=pl.ANY))
        args.append(y_so_far)
        aliases = {len(args) - 1: 0}
        body = lambda be, bv, xs_ref, w1_ref, w3_ref, w2_ref, y_prev, yb_ref, *scr: _expert_body(
            be, bv, xs_ref, w1_ref, w3_ref, w2_ref, yb_ref, *scr)
    grid_spec = pltpu.PrefetchScalarGridSpec(
        num_scalar_prefetch=2,
        grid=(n_blk,),
        in_specs=in_specs,
        out_specs=pl.BlockSpec((MOE_BLOCK, d), lambda i, be, nu: (i + blk0, 0)),
        scratch_shapes=[
            pltpu.VMEM((d, D_EXPERT), BF16),
            pltpu.VMEM((d, D_EXPERT), BF16),
            pltpu.VMEM((D_EXPERT, d), BF16),
        ],
    )
    return pl.pallas_call(
        body,
        grid_spec=grid_spec,
        out_shape=jax.ShapeDtypeStruct((n_blk_total * MOE_BLOCK, d), BF16),
        input_output_aliases=aliases,
        compiler_params=_cparams(("arbitrary",)),
        name="experts",
    )(*args)


def _combine_body(x_ref, y0_ref, y1_ref, rg_ref, o_ref):
    g = rg_ref[...]
    o_ref[...] = (x_ref[...] + g[:, 0:1] * y0_ref[...].astype(F32)
                  + g[:, 1:2] * y1_ref[...].astype(F32))


def _combine(xnew, y0, y1, rg, tm=1024):
    t, d = xnew.shape
    row = lambda width: pl.BlockSpec((tm, width), lambda i: (i, 0))
    return pl.pallas_call(
        _combine_body,
        grid=(t // tm,),
        in_specs=[row(d), row(d), row(d), row(ROUTE_W)],
        out_specs=row(d),
        out_shape=jax.ShapeDtypeStruct((t, d), F32),
        compiler_params=_cparams(("parallel",)),
        name="combine",
    )(xnew, y0, y1, rg)


def _moe_dispatch_plan(ri, cnt, t):
    n_blk = (t * TOP_K) // MOE_BLOCK + N_EXPERTS
    counts = cnt[0, :N_EXPERTS].astype(jnp.int32)
    padded = (counts + MOE_BLOCK - 1) // MOE_BLOCK * MOE_BLOCK
    pend = jnp.cumsum(padded)
    pstart = pend - padded
    expert_ids = jnp.arange(N_EXPERTS, dtype=jnp.int32)
    dest = ri[:, 2:4] + jnp.sum(jnp.where(ri[:, 0:2, None] == expert_ids, pstart, 0), axis=-1)
    blk_start = jnp.arange(n_blk, dtype=jnp.int32) * MOE_BLOCK
    blk_expert = jnp.minimum(jnp.sum((pend[None, :] <= blk_start[:, None]).astype(jnp.int32), axis=1),
                             N_EXPERTS - 1)
    blk_valid = jnp.clip((pstart + counts)[blk_expert] - blk_start, 0, MOE_BLOCK)
    return dest, blk_expert, blk_valid, n_blk


def _layer(xf, moe, layer, p, b, s):
    t = b * s
    x, q, k, v, ml, gcol, grow = _inproj(xf, moe, p, layer, b, s)
    ya = _attention(q, k, v, p["logit_bound"][layer])
    yml = _mlstm(ml, gcol, grow, p["conv_w"], p["b_col"], p["b_row"], layer, b, s)
    xnew, hn, ri, rg, cnt = _outproj_router(x, ya, yml, p["w_out"], p["g_ffn"], p["w_r"], p["b_r"], layer, b, s)
    dest, blk_expert, blk_valid, n_blk = _moe_dispatch_plan(ri, cnt, t)
    tok = jnp.broadcast_to(jnp.arange(t, dtype=jnp.int32)[:, None], (t, TOP_K))
    buf_tok = jnp.zeros((n_blk * MOE_BLOCK,), jnp.int32).at[dest.reshape(-1)].set(tok.reshape(-1))
    yb = None
    part = n_blk // MOE_PARTS
    for i in range(MOE_PARTS):
        blks = slice(i * part, (i + 1) * part)
        xs = hn[buf_tok[i * part * MOE_BLOCK:(i + 1) * part * MOE_BLOCK]]
        yb = _experts(xs, blk_expert[blks], blk_valid[blks], p["w1"], p["w3"], p["w2"], layer,
                      i * part, n_blk, yb)
    return xnew, (yb[dest[:, 0]], yb[dest[:, 1]], rg)


def _prep(g_norm_mix, w_in, b_igate, b_fgate, conv_w, g_q, g_k, w_out, g_norm_ffn,
          w_group, b_group, w_expert_router, b_expert_router, w1, w3, w2):
    w_in_b = w_in.astype(BF16)
    n_gate = 2 * N_HEADS_ML
    hd_id = jnp.arange(D_ATT, dtype=jnp.int32) // HD_ATT
    pad_r = LANES - N_GROUPS - N_EXPERTS
    w_r = jnp.pad(jnp.concatenate([w_group, w_expert_router], axis=-1), ((0, 0), (0, 0), (0, pad_r)))
    w_r_hi = w_r.astype(BF16)
    w_r_lo = (w_r - w_r_hi.astype(F32)).astype(BF16)
    p = {
        "g_mix": g_norm_mix[:, None, :],
        "w_att": w_in_b[:, :, :3 * D_ATT],
        "w_ml": w_in_b[:, :, 3 * D_ATT:3 * D_ATT + 4 * D_ML],
        "w_g": w_in_b[:, :, -n_gate:],
        "w_gt": jnp.swapaxes(w_in_b[:, :, -n_gate:], 1, 2),
        "bd": jnp.where(hd_id[:, None] == hd_id[None, :], 1.0 / HD_ATT, 0.0).astype(BF16),
        "gq": (jnp.tile(g_q, (1, N_HEADS_ATT)) * (HD_ATT ** -0.5))[:, None, :],
        "gk": jnp.tile(g_k, (1, N_HEADS_ATT))[:, None, :],
        "logit_bound": (1.02 * HD_ATT ** 0.5 * jnp.max(jnp.abs(g_q), axis=-1)
                        * jnp.max(jnp.abs(g_k), axis=-1))[:, None],
        "conv_w": conv_w,
        "b_col": jnp.concatenate([b_igate, b_fgate], axis=-1)[:, None, :],
        "b_row": jnp.concatenate([b_igate, b_fgate], axis=-1)[:, :, None],
        "w_out": w_out.astype(BF16),
        "g_ffn": g_norm_ffn[:, None, :],
        "w_r": jnp.concatenate([w_r_hi, w_r_lo], axis=-1),
        "b_r": jnp.pad(jnp.concatenate([b_group, b_expert_router], axis=-1), ((0, 0), (0, pad_r)))[:, None, :],
        "w1": w1, "w3": w3, "w2": w2,
    }
    return p


def kernel(x, g_norm_mix, w_in, b_igate, b_fgate, conv_w, g_q, g_k, w_out, g_norm_ffn,
           w_group, b_group, w_expert_router, b_expert_router, w1, w3, w2):
    p = _prep(g_norm_mix, w_in, b_igate, b_fgate, conv_w, g_q, g_k, w_out, g_norm_ffn,
              w_group, b_group, w_expert_router, b_expert_router, w1, w3, w2)
    b, s, d = x.shape
    xf, moe = x.reshape(b * s, d), None
    for layer in range(w_in.shape[0]):
        xf, moe = _layer(xf, moe, layer, p, b, s)
    return _combine(xf, *moe).reshape(b, s, d)
```

```python
import functools

import jax
import jax.numpy as jnp
from jax import lax
from jax.experimental import pallas as pl
from jax.experimental.pallas import tpu as pltpu

F32 = jnp.float32
BF16 = jnp.bfloat16

EPS = 1e-6
D_MODEL = 1024
N_HEADS_ATT = 8
HD_ATT = 64
D_ATT = N_HEADS_ATT * HD_ATT
N_PAIRS = D_ATT // 128
DILATIONS = (1, 4, 16)
BAND = 128
ATT_TILE = BAND * 16
N_HEADS_ML = 4
HD_ML = 128
D_ML = N_HEADS_ML * HD_ML
CONV_W = 4
CONV_HALO = 16
ML_CHUNK = 256
N_GROUPS = 4
EXPERTS_PER_GROUP = 8
N_EXPERTS = N_GROUPS * EXPERTS_PER_GROUP
TOP_K = 2
D_EXPERT = 512
MOE_BLOCK = 256
MOE_PARTS = 4
LANES = 128
ROUTE_W = 8
NEG = -1e30
VMEM_LIMIT = 56 * 1024 * 1024


def _cparams(sem):
    return pltpu.CompilerParams(dimension_semantics=sem, vmem_limit_bytes=VMEM_LIMIT)


def _inproj_body(*refs, fused_combine):
    if fused_combine:
        (x_ref, y0_ref, y1_ref, rg_ref, g_ref, watt_ref, wml_ref, wg_ref, wgt_ref, bd_ref, gq_ref, gk_ref,
         xo_ref, q_ref, k_ref, v_ref, ml_ref, gcol_ref, grow_ref) = refs
        rg = rg_ref[...]
        x = (x_ref[...] + rg[:, 0:1] * y0_ref[...].astype(F32) + rg[:, 1:2] * y1_ref[...].astype(F32))
        xo_ref[...] = x
    else:
        (x_ref, g_ref, watt_ref, wml_ref, wg_ref, wgt_ref, bd_ref, gq_ref, gk_ref,
         q_ref, k_ref, v_ref, ml_ref, gcol_ref, grow_ref) = refs
        x = x_ref[...]
    ms = jnp.mean(x * x, axis=-1, keepdims=True)
    xn = (x * lax.rsqrt(ms + EPS) * g_ref[...]).astype(BF16)
    att = jnp.dot(xn, watt_ref[...], preferred_element_type=F32)

    def headnorm(t, gain):
        msq = jnp.dot((t * t).astype(BF16), bd_ref[...], preferred_element_type=F32)
        return t * lax.rsqrt(msq + EPS) * gain

    q = headnorm(att[:, :D_ATT], gq_ref[...])
    k = headnorm(att[:, D_ATT:2 * D_ATT], gk_ref[...])
    v = att[:, 2 * D_ATT:]
    for p in range(N_PAIRS):
        sl = slice(LANES * p, LANES * (p + 1))
        q_ref[p] = q[:, sl].astype(BF16)
        k_ref[p] = k[:, sl].astype(BF16)
        v_ref[p] = v[:, sl].astype(BF16)
    ml_ref[...] = jnp.dot(xn, wml_ref[...], preferred_element_type=F32).astype(BF16)
    gcol_ref[...] = jnp.dot(xn, wg_ref[...], preferred_element_type=F32)
    grow_ref[...] = lax.dot_general(wgt_ref[...], xn, (((1,), (1,)), ((), ())),
                                    preferred_element_type=F32)


def _inproj(xf, moe, p, layer, b, s, tm=512):
    d = xf.shape[1]
    ns = s // tm
    pair_spec = pl.BlockSpec((None, N_PAIRS, tm, LANES), lambda i: (i // ns, 0, i % ns, 0))
    pair_shape = jax.ShapeDtypeStruct((b, N_PAIRS, s, LANES), BF16)
    lw = lambda shape: pl.BlockSpec((None,) + shape, lambda i: (layer,) + (0,) * len(shape))
    row = lambda width: pl.BlockSpec((tm, width), lambda i: (i, 0))
    in_specs = [
        lw((1, d)), lw((d, 3 * D_ATT)), lw((d, 4 * D_ML)), lw((d, 2 * N_HEADS_ML)),
        lw((2 * N_HEADS_ML, d)),
        pl.BlockSpec((D_ATT, D_ATT), lambda i: (0, 0)),
        lw((1, D_ATT)), lw((1, D_ATT)),
    ]
    out_specs = [
        pair_spec, pair_spec, pair_spec,
        row(4 * D_ML), row(2 * N_HEADS_ML),
        pl.BlockSpec((None, 2 * N_HEADS_ML, tm), lambda i: (i // ns, 0, i % ns)),
    ]
    out_shape = [
        pair_shape, pair_shape, pair_shape,
        jax.ShapeDtypeStruct((b * s, 4 * D_ML), BF16),
        jax.ShapeDtypeStruct((b * s, 2 * N_HEADS_ML), F32),
        jax.ShapeDtypeStruct((b, 2 * N_HEADS_ML, s), F32),
    ]
    args = [p["g_mix"], p["w_att"], p["w_ml"], p["w_g"], p["w_gt"], p["bd"], p["gq"], p["gk"]]
    if moe is None:
        in_specs = [row(d)] + in_specs
        args = [xf] + args
    else:
        in_specs = [row(d), row(d), row(d), row(ROUTE_W)] + in_specs
        args = [xf, *moe] + args
        out_specs = [row(d)] + out_specs
        out_shape = [jax.ShapeDtypeStruct((b * s, d), F32)] + out_shape
    outs = pl.pallas_call(
        functools.partial(_inproj_body, fused_combine=moe is not None),
        grid=(b * ns,),
        in_specs=in_specs,
        out_specs=out_specs,
        out_shape=out_shape,
        compiler_params=_cparams(("parallel",)),
        name="inproj",
    )(*args)
    return outs if moe is not None else [xf] + list(outs)


ATT_SAFE_LOGIT_BOUND = 40.0


def _attn_body(bound_ref, q_ref, k_ref, v_ref, kp_ref, vp_ref, o_ref, qf, kf, vf, acc_ref, m_ref, l_ref,
               *, fixed_shift):
    tile = pl.program_id(2)
    lane = lax.broadcasted_iota(jnp.int32, (1, LANES), 1)
    first_head = lane < HD_ATT
    qi = lax.broadcasted_iota(jnp.int32, (BAND, 2 * BAND), 0)
    kj = lax.broadcasted_iota(jnp.int32, (BAND, 2 * BAND), 1)
    band = (kj >= qi) & (kj <= qi + BAND)
    shift = -bound_ref[0] if fixed_shift else 0.0
    bias_prev = jnp.where(band, shift, NEG).astype(F32)
    bias_first = jnp.where(band & ((kj >= BAND) | (tile > 0)), shift, NEG).astype(F32)

    qf[...] = q_ref[...].astype(F32)
    kf[0:ATT_TILE, :] = kp_ref[...].astype(F32)
    kf[ATT_TILE:, :] = k_ref[...].astype(F32)
    vf[0:ATT_TILE, :] = vp_ref[...].astype(F32)
    vf[ATT_TILE:, :] = v_ref[...].astype(F32)
    ones = jnp.ones((2 * BAND, LANES), BF16)

    def unit(d, r, n, mode):
        q0 = n * BAND * d + r
        k0 = ATT_TILE + (n - 1) * BAND * d + r
        rows = pl.ds(q0, BAND, stride=d) if d > 1 else pl.ds(q0, BAND)
        krows = pl.ds(k0, 2 * BAND, stride=d) if d > 1 else pl.ds(k0, 2 * BAND)
        q = qf[rows, :].astype(BF16)
        kcat = kf[krows, :].astype(BF16)
        vcat = vf[krows, :].astype(BF16)
        bias = bias_first if n == 0 else bias_prev
        if fixed_shift:
            vones = jnp.concatenate([vcat, ones], axis=1)
            outs = []
            for h in range(2):
                qh = jnp.where(first_head if h == 0 else ~first_head, q, jnp.zeros_like(q))
                s = lax.dot_general(qh, kcat, (((1,), (1,)), ((), ())), preferred_element_type=F32) + bias
                outs.append(jnp.dot(jnp.exp(s).astype(BF16), vones, preferred_element_type=F32))
            acc = jnp.where(first_head, outs[0][:, :LANES], outs[1][:, :LANES])
            l_pair = jnp.where(first_head, outs[0][:, LANES:], outs[1][:, LANES:])
            if mode != "init":
                acc = acc + acc_ref[rows, :]
                l_pair = l_pair + l_ref[rows, :]
            if mode == "final":
                o_ref[rows, :] = (acc / l_pair).astype(o_ref.dtype)
            else:
                l_ref[rows, :] = l_pair
                acc_ref[rows, :] = acc
            return
        if mode != "init":
            m_old = m_ref[rows, :]
            l_old = l_ref[rows, :]
        stats = []
        for h in range(2):
            qh = jnp.where(first_head if h == 0 else ~first_head, q, jnp.zeros_like(q))
            s = lax.dot_general(qh, kcat, (((1,), (1,)), ((), ())), preferred_element_type=F32) + bias
            m_new = jnp.max(s, axis=-1, keepdims=True)
            if mode != "init":
                m_new = jnp.maximum(m_new, m_old[:, h * HD_ATT:h * HD_ATT + 1])
            p = jnp.exp(s - m_new)
            l_new = jnp.sum(p, axis=-1, keepdims=True)
            pv = jnp.dot(p.astype(BF16), vcat, preferred_element_type=F32)
            stats.append((m_new, l_new, pv))
        (m0, l0, pv0), (m1, l1, pv1) = stats
        m_pair = jnp.where(first_head, m0, m1)
        l_pair = jnp.where(first_head, l0, l1)
        acc = jnp.where(first_head, pv0, pv1)
        if mode != "init":
            alpha = jnp.exp(m_old - m_pair)
            l_pair = l_pair + alpha * l_old
            acc = acc + alpha * acc_ref[rows, :]
        if mode == "final":
            o_ref[rows, :] = (acc / l_pair).astype(o_ref.dtype)
        else:
            m_ref[rows, :] = m_pair
            l_ref[rows, :] = l_pair
            acc_ref[rows, :] = acc

    for d, mode in ((16, "init"), (4, "update"), (1, "final")):
        for r in range(d):
            for n in range(ATT_TILE // (BAND * d)):
                unit(d, r, n, mode)


def _attention(q, k, v, logit_bound):
    b, npair, s, _ = q.shape
    cur = pl.BlockSpec((None, None, ATT_TILE, LANES), lambda bi, p, i: (bi, p, i, 0))
    prev = pl.BlockSpec((None, None, ATT_TILE, LANES), lambda bi, p, i: (bi, p, jnp.maximum(i - 1, 0), 0))

    def run(fixed_shift):
        return pl.pallas_call(
            functools.partial(_attn_body, fixed_shift=fixed_shift),
            grid=(b, npair, s // ATT_TILE),
            in_specs=[pl.BlockSpec(memory_space=pltpu.SMEM), cur, cur, cur, prev, prev],
            out_specs=cur,
            out_shape=jax.ShapeDtypeStruct((b, npair, s, LANES), BF16),
            scratch_shapes=[
                pltpu.VMEM((ATT_TILE, LANES), F32),
                pltpu.VMEM((2 * ATT_TILE, LANES), F32),
                pltpu.VMEM((2 * ATT_TILE, LANES), F32),
                pltpu.VMEM((ATT_TILE, LANES), F32),
                pltpu.VMEM((ATT_TILE, LANES), F32),
                pltpu.VMEM((ATT_TILE, LANES), F32),
            ],
            compiler_params=_cparams(("parallel", "parallel", "parallel")),
            name="dilated_attn_fixed_shift" if fixed_shift else "dilated_attn",
        )(logit_bound, q, k, v, k, v)

    return lax.cond(logit_bound[0] <= ATT_SAFE_LOGIT_BOUND, lambda: run(True), lambda: run(False))


def _log_sigmoid(z):
    return jnp.minimum(z, 0.0) - jnp.log(1.0 + jnp.exp(-jnp.abs(z)))


def _mlstm_body(qk_ref, v_ref, o_ref, gcol_ref, grow_ref, convw_ref, bcol_ref, brow_ref,
                y_ref, xc_ref, c_ref, n_ref, m_ref):
    L = ML_CHUNK
    chunk = pl.program_id(1)

    @pl.when(chunk == 0)
    def _():
        xc_ref[0:CONV_HALO, :] = jnp.zeros((CONV_HALO, 2 * D_ML), F32)
        c_ref[...] = jnp.zeros_like(c_ref)
        n_ref[...] = jnp.zeros_like(n_ref)
        m_ref[...] = jnp.zeros_like(m_ref)

    xc_ref[CONV_HALO:CONV_HALO + L, :] = qk_ref[...].astype(F32)
    w = convw_ref[...]
    conv = jnp.zeros((L, 2 * D_ML), F32)
    for j in range(CONV_W):
        off = CONV_HALO - (CONV_W - 1) + j
        conv = conv + w[j:j + 1, :] * xc_ref[off:off + L, :]
    xc_ref[0:CONV_HALO, :] = xc_ref[L:L + CONV_HALO, :]
    qkc = conv * jax.nn.sigmoid(conv)

    gc = gcol_ref[...] + bcol_ref[...]
    gr = grow_ref[...] + brow_ref[...]
    i_col, f_col = gc[:, :N_HEADS_ML], _log_sigmoid(gc[:, N_HEADS_ML:])
    i_row, f_row = gr[:N_HEADS_ML, :], _log_sigmoid(gr[N_HEADS_ML:, :])
    t_idx = lax.broadcasted_iota(jnp.int32, (L, L), 0)
    s_idx = lax.broadcasted_iota(jnp.int32, (L, L), 1)
    causal = s_idx <= t_idx

    for h in range(N_HEADS_ML):
        ln = slice(h * HD_ML, (h + 1) * HD_ML)
        q = qkc[:, ln]
        k = qkc[:, D_ML + h * HD_ML:D_ML + (h + 1) * HD_ML] * (HD_ML ** -0.5)
        v = v_ref[:, ln]
        qb, kb = q.astype(BF16), k.astype(BF16)
        b_col = jnp.sum(jnp.where(causal, f_row[h:h + 1, :], 0.0), axis=1, keepdims=True)
        b_row = jnp.sum(jnp.where(t_idx <= s_idx, f_col[:, h:h + 1], 0.0), axis=0, keepdims=True)
        a_row = i_row[h:h + 1, :] - b_row
        a_col = i_col[:, h:h + 1] - b_col
        m_st = m_ref[h][0:1, 0:1]
        a_mat = jnp.where(causal, a_row, -jnp.inf)
        m_col = jnp.maximum(jnp.max(a_mat, axis=1, keepdims=True), m_st)
        w_intra = jnp.exp(a_mat - m_col)
        w_inter = jnp.exp(m_st - m_col)
        qk = lax.dot_general(qb, kb, (((1,), (1,)), ((), ())), preferred_element_type=F32) * w_intra
        c_st = c_ref[h]
        n_st = n_ref[h][0:1, :]
        num = (jnp.dot(qk.astype(BF16), v, preferred_element_type=F32)
               + w_inter * jnp.dot(qb, c_st.astype(BF16), preferred_element_type=F32))
        den = (jnp.sum(qk, axis=1, keepdims=True)
               + w_inter * jnp.sum(q * n_st, axis=1, keepdims=True))
        h_out = num / jnp.maximum(jnp.abs(den), jnp.exp(-(b_col + m_col)))
        m_last = m_col[L - 1:L, :]
        b_last = b_col[L - 1:L, :]
        w_s = jnp.exp(a_col - m_last)
        w_c = jnp.exp(m_st - m_last)
        vs = (w_s * v.astype(F32)).astype(BF16)
        c_ref[h] = w_c * c_st + lax.dot_general(kb, vs, (((0,), (0,)), ((), ())),
                                                preferred_element_type=F32)
        n_new = w_c * n_st + jnp.sum(w_s * k, axis=0, keepdims=True)
        n_ref[h] = jnp.broadcast_to(n_new, (8, HD_ML))
        m_ref[h] = jnp.broadcast_to(b_last + m_last, (8, LANES))
        y_ref[:, ln] = (jax.nn.sigmoid(o_ref[:, ln].astype(F32)) * h_out).astype(y_ref.dtype)


def _mlstm(ml, gcol, grow, convw, bcol, brow, layer, b, s):
    L = ML_CHUNK
    nc = s // L
    lw = lambda shape: pl.BlockSpec((None,) + shape, lambda bi, i: (layer,) + (0,) * len(shape))
    return pl.pallas_call(
        _mlstm_body,
        grid=(b, nc),
        in_specs=[
            pl.BlockSpec((L, 2 * D_ML), lambda bi, i: (bi * nc + i, 0)),
            pl.BlockSpec((L, D_ML), lambda bi, i: (bi * nc + i, 2)),
            pl.BlockSpec((L, D_ML), lambda bi, i: (bi * nc + i, 3)),
            pl.BlockSpec((L, 2 * N_HEADS_ML), lambda bi, i: (bi * nc + i, 0)),
            pl.BlockSpec((None, 2 * N_HEADS_ML, L), lambda bi, i: (bi, 0, i)),
            lw((CONV_W, 2 * D_ML)), lw((1, 2 * N_HEADS_ML)), lw((2 * N_HEADS_ML, 1)),
        ],
        out_specs=pl.BlockSpec((L, D_ML), lambda bi, i: (bi * nc + i, 0)),
        out_shape=jax.ShapeDtypeStruct((b * s, D_ML), BF16),
        scratch_shapes=[
            pltpu.VMEM((L + CONV_HALO, 2 * D_ML), F32),
            pltpu.VMEM((N_HEADS_ML, HD_ML, HD_ML), F32),
            pltpu.VMEM((N_HEADS_ML, 8, HD_ML), F32),
            pltpu.VMEM((N_HEADS_ML, 8, LANES), F32),
        ],
        compiler_params=_cparams(("parallel", "arbitrary")),
        name="mlstm",
    )(ml, ml, ml, gcol, grow, convw, bcol, brow)


def _outproj_body(x_ref, ya_ref, yml_ref, wout_ref, g_ref, wr_ref, br_ref,
                  xnew_ref, hn_ref, ri_ref, rg_ref, cnt_ref):
    tm = x_ref.shape[0]

    @pl.when(pl.program_id(0) == 0)
    def _():
        cnt_ref[...] = jnp.zeros_like(cnt_ref)

    y = jnp.concatenate([ya_ref[p] for p in range(N_PAIRS)] + [yml_ref[...]], axis=-1)
    xnew = x_ref[...] + jnp.dot(y, wout_ref[...], preferred_element_type=F32)
    xnew_ref[...] = xnew
    ms = jnp.mean(xnew * xnew, axis=-1, keepdims=True)
    hn = xnew * lax.rsqrt(ms + EPS) * g_ref[...]
    hn_ref[...] = hn.astype(hn_ref.dtype)

    hn_hi = hn.astype(BF16)
    hn_lo = (hn - hn_hi.astype(F32)).astype(BF16)
    hi_prod = jnp.dot(hn_hi, wr_ref[...], preferred_element_type=F32)
    logits = (hi_prod[:, :LANES]
              + (jnp.dot(hn_lo, wr_ref[:, :LANES], preferred_element_type=F32) + hi_prod[:, LANES:])
              ) + br_ref[...]
    lane = lax.broadcasted_iota(jnp.int32, (tm, LANES), 1)
    lane_f = lane.astype(F32)
    big = float(LANES)

    def first_max(vals, valid):
        masked = jnp.where(valid, vals, -jnp.inf)
        top = jnp.max(masked, axis=-1, keepdims=True)
        idx = jnp.min(jnp.where(valid & (masked == top), lane_f, big), axis=-1, keepdims=True)
        return top, idx

    is_group = lane < N_GROUPS
    g_top, g_sel = first_max(logits, is_group)
    g_gate = 1.0 / jnp.sum(jnp.where(is_group, jnp.exp(logits - g_top), 0.0), axis=-1, keepdims=True)
    lo = N_GROUPS + EXPERTS_PER_GROUP * g_sel
    in_group = (lane_f >= lo) & (lane_f < lo + EXPERTS_PER_GROUP)
    t1, i1 = first_max(logits, in_group)
    t2, i2 = first_max(logits, in_group & (lane_f != i1))
    r = jnp.exp(t2 - t1)
    p1 = 1.0 / (1.0 + r)
    e1 = i1 - N_GROUPS
    e2 = i2 - N_GROUPS

    hit1 = lane_f == e1
    hit2 = lane_f == e2
    onehot = jnp.where(hit1 | hit2, 1.0, 0.0)
    ti = lax.broadcasted_iota(jnp.int32, (tm, tm), 0)
    tj = lax.broadcasted_iota(jnp.int32, (tm, tm), 1)
    strict_lower = jnp.where(tj < ti, 1.0, 0.0).astype(BF16)
    before = jnp.dot(strict_lower, onehot.astype(BF16), preferred_element_type=F32) + cnt_ref[...]
    rank1 = jnp.sum(jnp.where(hit1, before, 0.0), axis=-1, keepdims=True)
    rank2 = jnp.sum(jnp.where(hit2, before, 0.0), axis=-1, keepdims=True)
    cnt_ref[...] = cnt_ref[...] + jnp.sum(onehot, axis=0, keepdims=True)

    ri = jnp.where(lane == 0, e1, jnp.where(lane == 1, e2, jnp.where(lane == 2, rank1,
                   jnp.where(lane == 3, rank2, 0.0))))
    rg = jnp.where(lane == 0, g_gate * p1, jnp.where(lane == 1, g_gate * r * p1, 0.0))
    ri_ref[...] = ri[:, :ROUTE_W].astype(jnp.int32)
    rg_ref[...] = rg[:, :ROUTE_W]


def _outproj_router(xf, ya, yml, wout, g, wr, br, layer, b, s, tm=512):
    t, d = xf.shape
    ns = s // tm
    lw = lambda shape: pl.BlockSpec((None,) + shape, lambda i: (layer,) + (0,) * len(shape))
    row = lambda width: pl.BlockSpec((tm, width), lambda i: (i, 0))
    return pl.pallas_call(
        _outproj_body,
        grid=(t // tm,),
        in_specs=[
            row(d),
            pl.BlockSpec((None, N_PAIRS, tm, LANES), lambda i: (i // ns, 0, i % ns, 0)),
            row(D_ML),
            lw((d, d)), lw((1, d)), lw((d, 2 * LANES)), lw((1, LANES)),
        ],
        out_specs=[row(d), row(d), row(ROUTE_W), row(ROUTE_W), pl.BlockSpec((1, LANES), lambda i: (0, 0))],
        out_shape=[
            jax.ShapeDtypeStruct((t, d), F32),
            jax.ShapeDtypeStruct((t, d), BF16),
            jax.ShapeDtypeStruct((t, ROUTE_W), jnp.int32),
            jax.ShapeDtypeStruct((t, ROUTE_W), F32),
            jax.ShapeDtypeStruct((1, LANES), F32),
        ],
        compiler_params=_cparams(("arbitrary",)),
        name="outproj_router",
    )(xf, ya, yml, wout, g, wr, br)


def _expert_body(be_ref, bv_ref, xs_ref, w1_ref, w3_ref, w2_ref, yb_ref, w1b, w3b, w2b):
    i = pl.program_id(0)
    e = be_ref[i]
    e_prev = be_ref[jnp.maximum(i - 1, 0)]
    valid = bv_ref[i]

    @pl.when((i == 0) | (e != e_prev))
    def _():
        w1b[...] = w1_ref[...].astype(BF16)
        w3b[...] = w3_ref[...].astype(BF16)
        w2b[...] = w2_ref[...].astype(BF16)

    @pl.when(valid > 0)
    def _():
        x = xs_ref[...]
        a = jnp.dot(x, w1b[...], preferred_element_type=F32)
        g = jnp.dot(x, w3b[...], preferred_element_type=F32)
        hdn = (a * jax.nn.sigmoid(a) * g).astype(BF16)
        yb_ref[...] = jnp.dot(hdn, w2b[...], preferred_element_type=F32).astype(yb_ref.dtype)

    @pl.when(valid == 0)
    def _():
        yb_ref[...] = jnp.zeros_like(yb_ref)


def _experts(xs, blk_expert, blk_valid, w1, w3, w2, layer, blk0, n_blk_total, y_so_far=None):
    rows, d = xs.shape
    n_blk = rows // MOE_BLOCK
    in_specs = [
        pl.BlockSpec((MOE_BLOCK, d), lambda i, be, nu: (i, 0)),
        pl.BlockSpec((None, None, d, D_EXPERT), lambda i, be, nu: (layer, be[i], 0, 0)),
        pl.BlockSpec((None, None, d, D_EXPERT), lambda i, be, nu: (layer, be[i], 0, 0)),
        pl.BlockSpec((None, None, D_EXPERT, d), lambda i, be, nu: (layer, be[i], 0, 0)),
    ]
    args = [blk_expert, blk_valid, xs, w1, w3, w2]
    aliases = {}
    body = _expert_body
    if y_so_far is not None:
        in_specs.append(pl.BlockSpec(memory_space=pl.ANY))
        args.append(y_so_far)
        aliases = {len(args) - 1: 0}
        body = lambda be, bv, xs_ref, w1_ref, w3_ref, w2_ref, y_prev, yb_ref, *scr: _expert_body(
            be, bv, xs_ref, w1_ref, w3_ref, w2_ref, yb_ref, *scr)
    grid_spec = pltpu.PrefetchScalarGridSpec(
        num_scalar_prefetch=2,
        grid=(n_blk,),
        in_specs=in_specs,
        out_specs=pl.BlockSpec((MOE_BLOCK, d), lambda i, be, nu: (i + blk0, 0)),
        scratch_shapes=[
            pltpu.VMEM((d, D_EXPERT), BF16),
            pltpu.VMEM((d, D_EXPERT), BF16),
            pltpu.VMEM((D_EXPERT, d), BF16),
        ],
    )
    return pl.pallas_call(
        body,
        grid_spec=grid_spec,
        out_shape=jax.ShapeDtypeStruct((n_blk_total * MOE_BLOCK, d), BF16),
        input_output_aliases=aliases,
        compiler_params=_cparams(("arbitrary",)),
        name="experts",
    )(*args)


def _combine_body(x_ref, y0_ref, y1_ref, rg_ref, o_ref):
    g = rg_ref[...]
    o_ref[...] = (x_ref[...] + g[:, 0:1] * y0_ref[...].astype(F32)
                  + g[:, 1:2] * y1_ref[...].astype(F32))


def _combine(xnew, y0, y1, rg, tm=1024):
    t, d = xnew.shape
    row = lambda width: pl.BlockSpec((tm, width), lambda i: (i, 0))
    return pl.pallas_call(
        _combine_body,
        grid=(t // tm,),
        in_specs=[row(d), row(d), row(d), row(ROUTE_W)],
        out_specs=row(d),
        out_shape=jax.ShapeDtypeStruct((t, d), F32),
        compiler_params=_cparams(("parallel",)),
        name="combine",
    )(xnew, y0, y1, rg)


def _moe_dispatch_plan(ri, cnt, t):
    n_blk = (t * TOP_K) // MOE_BLOCK + N_EXPERTS
    counts = cnt[0, :N_EXPERTS].astype(jnp.int32)
    padded = (counts + MOE_BLOCK - 1) // MOE_BLOCK * MOE_BLOCK
    pend = jnp.cumsum(padded)
    pstart = pend - padded
    expert_ids = jnp.arange(N_EXPERTS, dtype=jnp.int32)
    dest = ri[:, 2:4] + jnp.sum(jnp.where(ri[:, 0:2, None] == expert_ids, pstart, 0), axis=-1)
    blk_start = jnp.arange(n_blk, dtype=jnp.int32) * MOE_BLOCK
    blk_expert = jnp.minimum(jnp.sum((pend[None, :] <= blk_start[:, None]).astype(jnp.int32), axis=1),
                             N_EXPERTS - 1)
    blk_valid = jnp.clip((pstart + counts)[blk_expert] - blk_start, 0, MOE_BLOCK)
    return dest, blk_expert, blk_valid, n_blk


def _layer(xf, moe, layer, p, b, s):
    t = b * s
    x, q, k, v, ml, gcol, grow = _inproj(xf, moe, p, layer, b, s)
    ya = _attention(q, k, v, p["logit_bound"][layer])
    yml = _mlstm(ml, gcol, grow, p["conv_w"], p["b_col"], p["b_row"], layer, b, s)
    xnew, hn, ri, rg, cnt = _outproj_router(x, ya, yml, p["w_out"], p["g_ffn"], p["w_r"], p["b_r"], layer, b, s)
    dest, blk_expert, blk_valid, n_blk = _moe_dispatch_plan(ri, cnt, t)
    tok = jnp.broadcast_to(jnp.arange(t, dtype=jnp.int32)[:, None], (t, TOP_K))
    buf_tok = (jnp.arange(n_blk * MOE_BLOCK, dtype=jnp.int32) % t).at[dest.reshape(-1)].set(tok.reshape(-1))
    yb = None
    part = n_blk // MOE_PARTS
    for i in range(MOE_PARTS):
        blks = slice(i * part, (i + 1) * part)
        xs = hn[buf_tok[i * part * MOE_BLOCK:(i + 1) * part * MOE_BLOCK]]
        yb = _experts(xs, blk_expert[blks], blk_valid[blks], p["w1"], p["w3"], p["w2"], layer,
                      i * part, n_blk, yb)
    return xnew, (yb[dest[:, 0]], yb[dest[:, 1]], rg)


def _prep(g_norm_mix, w_in, b_igate, b_fgate, conv_w, g_q, g_k, w_out, g_norm_ffn,
          w_group, b_group, w_expert_router, b_expert_router, w1, w3, w2):
    w_in_b = w_in.astype(BF16)
    n_gate = 2 * N_HEADS_ML
    hd_id = jnp.arange(D_ATT, dtype=jnp.int32) // HD_ATT
    pad_r = LANES - N_GROUPS - N_EXPERTS
    w_r = jnp.pad(jnp.concatenate([w_group, w_expert_router], axis=-1), ((0, 0), (0, 0), (0, pad_r)))
    w_r_hi = w_r.astype(BF16)
    w_r_lo = (w_r - w_r_hi.astype(F32)).astype(BF16)
    p = {
        "g_mix": g_norm_mix[:, None, :],
        "w_att": w_in_b[:, :, :3 * D_ATT],
        "w_ml": w_in_b[:, :, 3 * D_ATT:3 * D_ATT + 4 * D_ML],
        "w_g": w_in_b[:, :, -n_gate:],
        "w_gt": jnp.swapaxes(w_in_b[:, :, -n_gate:], 1, 2),
        "bd": jnp.where(hd_id[:, None] == hd_id[None, :], 1.0 / HD_ATT, 0.0).astype(BF16),
        "gq": (jnp.tile(g_q, (1, N_HEADS_ATT)) * (HD_ATT ** -0.5))[:, None, :],
        "gk": jnp.tile(g_k, (1, N_HEADS_ATT))[:, None, :],
        "logit_bound": (1.02 * HD_ATT ** 0.5 * jnp.max(jnp.abs(g_q), axis=-1)
                        * jnp.max(jnp.abs(g_k), axis=-1))[:, None],
        "conv_w": conv_w,
        "b_col": jnp.concatenate([b_igate, b_fgate], axis=-1)[:, None, :],
        "b_row": jnp.concatenate([b_igate, b_fgate], axis=-1)[:, :, None],
        "w_out": w_out.astype(BF16),
        "g_ffn": g_norm_ffn[:, None, :],
        "w_r": jnp.concatenate([w_r_hi, w_r_lo], axis=-1),
        "b_r": jnp.pad(jnp.concatenate([b_group, b_expert_router], axis=-1), ((0, 0), (0, pad_r)))[:, None, :],
        "w1": w1, "w3": w3, "w2": w2,
    }
    return p


def kernel(x, g_norm_mix, w_in, b_igate, b_fgate, conv_w, g_q, g_k, w_out, g_norm_ffn,
           w_group, b_group, w_expert_router, b_expert_router, w1, w3, w2):
    p = _prep(g_norm_mix, w_in, b_igate, b_fgate, conv_w, g_q, g_k, w_out, g_norm_ffn,
              w_group, b_group, w_expert_router, b_expert_router, w1, w3, w2)
    b, s, d = x.shape
    xf, moe = x.reshape(b * s, d), None
    for layer in range(w_in.shape[0]):
        xf, moe = _layer(xf, moe, layer, p, b, s)
    return _combine(xf, *moe).reshape(b, s, d)
```
